```python
import jax, jax.numpy as jnp
from jax import lax
import numpy as np

D_MODEL = 1024
BATCH = 4
SEQ = 4096
DEPTH = 2
DEC_BATCH = 32
DEC_SEQ = 1
PAST_LEN = 16384
PAGE_SIZE = 128

SB_HEADS = 16
SB_HEAD_DIM = D_MODEL // SB_HEADS
Q_BLOCK = 128
SB_BIAS_INIT = -8.0
SSD_EXPAND = 2
SSD_D_INNER = SSD_EXPAND * D_MODEL
SSD_HEAD_DIM = 64
SSD_HEADS = SSD_D_INNER // SSD_HEAD_DIM
SSD_GROUPS = 4
SSD_D_STATE = 128
SSD_CONV_W = 4
SSD_CONV_DIM = SSD_D_INNER + 2 * SSD_GROUPS * SSD_D_STATE
SSD_D_IN_PROJ = SSD_D_INNER + SSD_CONV_DIM + SSD_HEADS
SSD_CHUNK = 128
D_FF = 2816
N_EXPERTS = 8
TOP_K = 2
D_FF_EXPERT = 3584
EPS = 1e-6

kernel_name = "sb_ssd_hybrid_decode_step"


def rmsnorm(x, g):
    x32 = x.astype(jnp.float32)
    y = x32 * lax.rsqrt(jnp.mean(x32 * x32, axis=-1, keepdims=True) + EPS)
    return (y * g.astype(jnp.float32)).astype(x.dtype)


def adaln(c, w_ada, b_ada):
    mod = jax.nn.silu(c) @ w_ada + b_ada
    return jnp.split(mod[:, None, :], 6, axis=-1)


def modulate(x, shift, scale):
    return x * (1.0 + scale) + shift


def sb_segment(q, k, v, q_pos, k_pos, tail, bias):
    z = jnp.einsum('bqhd,bkhd->bhqk', q, k, preferred_element_type=jnp.float32) * (SB_HEAD_DIM ** -0.5)
    z = z + bias.astype(jnp.float32)[None, :, None, None]
    mask = k_pos[None, :] < q_pos[:, None]
    sp = jnp.where(mask, jax.nn.softplus(z), 0.0)
    r_incl = lax.cumsum(sp, axis=3, reverse=True)
    log_w = jax.nn.log_sigmoid(z) - (r_incl - sp + tail[..., None])
    w = jnp.where(mask, jnp.exp(log_w), 0.0)
    out = jnp.einsum('bhqk,bkhd->bqhd', w.astype(v.dtype), v)
    return out, r_incl[..., 0]


def sb_project(h, w_qkv):
    b, t, _ = h.shape
    q, k, v = jnp.split(h @ w_qkv, 3, axis=-1)
    shp = (b, t, SB_HEADS, SB_HEAD_DIM)
    return q.reshape(shp), k.reshape(shp), v.reshape(shp)


def sb_prompt(h, w_qkv, w_o, bias):
    b, s, _ = h.shape
    q, k, v = sb_project(h, w_qkv)
    nb = s // Q_BLOCK
    q_blocks = jnp.moveaxis(q.reshape(b, nb, Q_BLOCK, SB_HEADS, SB_HEAD_DIM), 1, 0)
    pos_blocks = jnp.arange(s, dtype=jnp.int32).reshape(nb, Q_BLOCK)
    k_pos = jnp.arange(s, dtype=jnp.int32)
    tail = jnp.zeros((b, SB_HEADS, Q_BLOCK), jnp.float32)

    def one_block(args):
        qb, pb = args
        return sb_segment(qb, k, v, pb, k_pos, tail, bias)[0]

    o = lax.map(one_block, (q_blocks, pos_blocks))
    o = jnp.moveaxis(o, 0, 1).reshape(b, s, D_MODEL)
    return o @ w_o, k, v


def sb_sample(h, cache_k, cache_v, page_table, w_qkv, w_o, bias):
    b, t, _ = h.shape
    q, k, v = sb_project(h, w_qkv)
    past_len = page_table.shape[1] * PAGE_SIZE
    k_past = cache_k[page_table].reshape(b, past_len, SB_HEADS, SB_HEAD_DIM)
    v_past = cache_v[page_table].reshape(b, past_len, SB_HEADS, SB_HEAD_DIM)
    q_pos = past_len + jnp.arange(t, dtype=jnp.int32)
    tail0 = jnp.zeros((b, SB_HEADS, t), jnp.float32)
    o_new, tail = sb_segment(q, k, v, q_pos, q_pos, tail0, bias)
    o_past, _ = sb_segment(q, k_past, v_past, q_pos, jnp.arange(past_len, dtype=jnp.int32), tail, bias)
    o = (o_new + o_past).reshape(b, t, D_MODEL)
    return o @ w_o, k, v


def ssd_in(h, w_in, conv_w, conv_b, conv_prefix, dt_bias):
    b, t, _ = h.shape
    zxbcdt = h @ w_in
    z, xbc, dt_raw = jnp.split(zxbcdt, [SSD_D_INNER, SSD_D_INNER + SSD_CONV_DIM], axis=-1)
    xbc_full = jnp.concatenate([conv_prefix.astype(xbc.dtype), xbc], axis=1)
    conv = lax.conv_general_dilated(
        xbc_full, conv_w[:, None, :].astype(xbc.dtype), window_strides=(1,), padding='VALID',
        dimension_numbers=('NWC', 'WIO', 'NWC'), feature_group_count=SSD_CONV_DIM) + conv_b
    xbc_act = jax.nn.silu(conv)
    x, bm, cm = jnp.split(xbc_act, [SSD_D_INNER, SSD_D_INNER + SSD_GROUPS * SSD_D_STATE], axis=-1)
    x = x.reshape(b, t, SSD_HEADS, SSD_HEAD_DIM)
    bm = bm.reshape(b, t, SSD_GROUPS, SSD_D_STATE)
    cm = cm.reshape(b, t, SSD_GROUPS, SSD_D_STATE)
    dt = jax.nn.softplus(dt_raw.astype(jnp.float32) + dt_bias.astype(jnp.float32))
    new_conv = xbc_full[:, -(SSD_CONV_W - 1):]
    return z, x, bm, cm, dt, new_conv


def ssd_chunked(x, dt, a, bm, cm):
    b, l, h, p = x.shape
    g, n = SSD_GROUPS, SSD_D_STATE
    r = h // g
    nc = l // SSD_CHUNK
    xg = x.astype(jnp.float32).reshape(b, nc, SSD_CHUNK, g, r, p)
    dtg = dt.reshape(b, nc, SSD_CHUNK, g, r)
    bg = bm.astype(jnp.float32).reshape(b, nc, SSD_CHUNK, g, n)
    cg = cm.astype(jnp.float32).reshape(b, nc, SSD_CHUNK, g, n)
    a_cs = jnp.cumsum(dtg * a.reshape(g, r), axis=2)
    seg = a_cs[:, :, :, None] - a_cs[:, :, None, :]
    tril = jnp.tril(jnp.ones((SSD_CHUNK, SSD_CHUNK), dtype=bool))
    decay = jnp.exp(jnp.where(tril[:, :, None, None], seg, -jnp.inf))
    cb = jnp.einsum('bclgn,bcsgn->bclsg', cg, bg)
    w = cb[..., None] * decay * dtg[:, :, None]
    y_diag = jnp.einsum('bclsgr,bcsgrp->bclgrp', w, xg)
    decay_states = jnp.exp(a_cs[:, :, -1:] - a_cs)
    xw = xg * (decay_states * dtg)[..., None]
    states = jnp.einsum('bclgn,bclgrp->bcgrpn', bg, xw)
    chunk_decay = jnp.exp(a_cs[:, :, -1])

    def step(carry, inp):
        s_c, d_c = inp
        return carry * d_c[..., None, None] + s_c, carry

    h0 = jnp.zeros((b, g, r, p, n), jnp.float32)
    final, prevs = lax.scan(step, h0, (jnp.moveaxis(states, 1, 0), jnp.moveaxis(chunk_decay, 1, 0)))
    prevs = jnp.moveaxis(prevs, 0, 1)
    y_off = jnp.einsum('bclgn,bcgrpn->bclgrp', cg, prevs) * jnp.exp(a_cs)[..., None]
    y = (y_diag + y_off).reshape(b, l, h, p)
    return y, final.reshape(b, h, p, n)


def ssd_recurrent(x, dt, a, bm, cm, h0):
    r = SSD_HEADS // SSD_GROUPS
    bh = jnp.repeat(bm.astype(jnp.float32), r, axis=2)
    ch = jnp.repeat(cm.astype(jnp.float32), r, axis=2)

    def step(hs, inp):
        xt, dtt, bt, ct = inp
        hs = hs * jnp.exp(dtt * a)[..., None, None] + (dtt[..., None] * xt)[..., None] * bt[:, :, None, :]
        return hs, jnp.einsum('bhpn,bhn->bhp', hs, ct)

    h_fin, ys = lax.scan(step, h0.astype(jnp.float32),
                         (jnp.moveaxis(x.astype(jnp.float32), 1, 0), jnp.moveaxis(dt, 1, 0),
                          jnp.moveaxis(bh, 1, 0), jnp.moveaxis(ch, 1, 0)))
    return jnp.moveaxis(ys, 0, 1), h_fin


def ssd_out(y, x, z, d_skip, norm_g, w_out):
    b, t = y.shape[:2]
    y = y + x.astype(jnp.float32) * d_skip.astype(jnp.float32)[:, None]
    y = y.reshape(b, t, SSD_D_INNER) * jax.nn.silu(z.astype(jnp.float32))
    yg = y.reshape(b, t, SSD_GROUPS, SSD_D_INNER // SSD_GROUPS)
    yg = yg * lax.rsqrt(jnp.mean(yg * yg, axis=-1, keepdims=True) + EPS)
    y = yg.reshape(b, t, SSD_D_INNER) * norm_g.astype(jnp.float32)
    return y.astype(z.dtype) @ w_out


def swiglu(h, w_gu, w_dn):
    g, u = jnp.split(h @ w_gu, 2, axis=-1)
    return (jax.nn.silu(g) * u) @ w_dn


def moe_ffn(h, w_router, w_gu, w_dn):
    logits = jnp.einsum('btd,de->bte', h, w_router, preferred_element_type=jnp.float32)
    top_v, top_i = lax.top_k(logits, TOP_K)
    gates = jax.nn.softmax(top_v, axis=-1)
    combine = jnp.sum(jax.nn.one_hot(top_i, N_EXPERTS, dtype=jnp.float32) * gates[..., None], axis=-2)
    out = jnp.zeros(h.shape, jnp.float32)
    for e in range(N_EXPERTS):
        out = out + combine[..., e:e + 1] * swiglu(h, w_gu[e], w_dn[e]).astype(jnp.float32)
    return out.astype(h.dtype)


def setup_inputs(seed: int = 0) -> dict:
    key = jax.random.key(seed)
    ks = jax.random.split(key, 32)
    f32 = jnp.float32

    def nrm(k, shape, scale):
        return jax.random.normal(k, shape, f32) * scale

    n_pages = PAST_LEN // PAGE_SIZE
    n_used = DEC_BATCH * n_pages
    n_phys = n_used + max(1, n_used // 4)
    page_table = jax.random.permutation(ks[0], n_phys)[:n_used].reshape(DEC_BATCH, n_pages).astype(jnp.int32)
    dt0 = jnp.exp(jax.random.uniform(ks[1], (SSD_HEADS,), f32, np.log(1e-3), np.log(1e-1)))
    dt_bias = dt0 + jnp.log(-jnp.expm1(-dt0))
    a_log = jnp.log(jax.random.uniform(ks[2], (SSD_HEADS,), f32, 1.0, 16.0))
    return {
        "x_prompt": nrm(ks[3], (BATCH, SEQ, D_MODEL), 1.0),
        "x_sample": nrm(ks[4], (DEC_BATCH, DEC_SEQ, D_MODEL), 1.0),
        "cache_k": nrm(ks[5], (n_phys, PAGE_SIZE, SB_HEADS, SB_HEAD_DIM), 1.0),
        "cache_v": nrm(ks[6], (n_phys, PAGE_SIZE, SB_HEADS, SB_HEAD_DIM), 1.0),
        "state_ssm": nrm(ks[7], (DEC_BATCH, SSD_HEADS, SSD_HEAD_DIM, SSD_D_STATE), 0.1),
        "state_conv": nrm(ks[8], (DEC_BATCH, SSD_CONV_W - 1, SSD_CONV_DIM), 1.0),
        "page_table": page_table,
        "c_prompt": nrm(ks[9], (BATCH, D_MODEL), 1.0),
        "c_sample": nrm(ks[10], (DEC_BATCH, D_MODEL), 1.0),
        "norm_g": 1.0 + nrm(ks[11], (DEPTH, 4, D_MODEL), 0.1),
        "w_ada": nrm(ks[12], (DEPTH, D_MODEL, 6 * D_MODEL), 0.5 * D_MODEL ** -0.5),
        "b_ada": nrm(ks[13], (DEPTH, 6 * D_MODEL), 0.02),
        "sb_w_qkv": nrm(ks[14], (D_MODEL, 3 * D_MODEL), D_MODEL ** -0.5),
        "sb_w_o": nrm(ks[15], (D_MODEL, D_MODEL), D_MODEL ** -0.5),
        "sb_logit_bias": SB_BIAS_INIT + nrm(ks[27], (SB_HEADS,), 0.5),
        "ssd_w_in": nrm(ks[16], (D_MODEL, SSD_D_IN_PROJ), D_MODEL ** -0.5),
        "ssd_conv_w": nrm(ks[17], (SSD_CONV_W, SSD_CONV_DIM), SSD_CONV_W ** -0.5),
        "ssd_conv_b": nrm(ks[18], (SSD_CONV_DIM,), 0.02),
        "ssd_dt_bias": dt_bias,
        "ssd_a_log": a_log,
        "ssd_d": 1.0 + nrm(ks[19], (SSD_HEADS,), 0.1),
        "ssd_norm_g": 1.0 + nrm(ks[20], (SSD_D_INNER,), 0.1),
        "ssd_w_out": nrm(ks[21], (SSD_D_INNER, D_MODEL), SSD_D_INNER ** -0.5),
        "ffn_w_gu": nrm(ks[22], (D_MODEL, 2 * D_FF), D_MODEL ** -0.5),
        "ffn_w_dn": nrm(ks[23], (D_FF, D_MODEL), D_FF ** -0.5),
        "moe_w_router": nrm(ks[24], (D_MODEL, N_EXPERTS), D_MODEL ** -0.5),
        "moe_w_gu": nrm(ks[25], (N_EXPERTS, D_MODEL, 2 * D_FF_EXPERT), D_MODEL ** -0.5),
        "moe_w_dn": nrm(ks[26], (N_EXPERTS, D_FF_EXPERT, D_MODEL), D_FF_EXPERT ** -0.5),
    }


def reference(x_prompt, x_sample, cache_k, cache_v, state_ssm, state_conv, page_table, c_prompt, c_sample,
              norm_g, w_ada, b_ada, sb_w_qkv, sb_w_o, sb_logit_bias, ssd_w_in, ssd_conv_w, ssd_conv_b, ssd_dt_bias,
              ssd_a_log, ssd_d, ssd_norm_g, ssd_w_out, ffn_w_gu, ffn_w_dn, moe_w_router, moe_w_gu, moe_w_dn):
    xp, xs = x_prompt, x_sample
    a_ssd = -jnp.exp(ssd_a_log.astype(jnp.float32))
    for i in range(DEPTH):
        mp_sh, mp_sc, mp_g, fp_sh, fp_sc, fp_g = adaln(c_prompt, w_ada[i], b_ada[i])
        ms_sh, ms_sc, ms_g, fs_sh, fs_sc, fs_g = adaln(c_sample, w_ada[i], b_ada[i])
        hp = modulate(rmsnorm(xp, norm_g[i, 0]), mp_sh, mp_sc)
        hs = modulate(rmsnorm(xs, norm_g[i, 0]), ms_sh, ms_sc)
        if i % 2 == 0:
            yp, k_prompt, v_prompt = sb_prompt(hp, sb_w_qkv, sb_w_o, sb_logit_bias)
            ys, k_sample, v_sample = sb_sample(hs, cache_k, cache_v, page_table, sb_w_qkv, sb_w_o, sb_logit_bias)
        else:
            prefix_p = jnp.zeros((hp.shape[0], SSD_CONV_W - 1, SSD_CONV_DIM), hp.dtype)
            z, x, bm, cm, dt, conv_prompt = ssd_in(hp, ssd_w_in, ssd_conv_w, ssd_conv_b, prefix_p, ssd_dt_bias)
            y, ssm_prompt = ssd_chunked(x, dt, a_ssd, bm, cm)
            yp = ssd_out(y, x, z, ssd_d, ssd_norm_g, ssd_w_out)
            z, x, bm, cm, dt, conv_sample = ssd_in(hs, ssd_w_in, ssd_conv_w, ssd_conv_b, state_conv, ssd_dt_bias)
            y, ssm_sample = ssd_recurrent(x, dt, a_ssd, bm, cm, state_ssm)
            ys = ssd_out(y, x, z, ssd_d, ssd_norm_g, ssd_w_out)
        xp = xp + mp_g * rmsnorm(yp, norm_g[i, 1])
        xs = xs + ms_g * rmsnorm(ys, norm_g[i, 1])
        hp = modulate(rmsnorm(xp, norm_g[i, 2]), fp_sh, fp_sc)
        hs = modulate(rmsnorm(xs, norm_g[i, 2]), fs_sh, fs_sc)
        if i % 2 == 0:
            yp = swiglu(hp, ffn_w_gu, ffn_w_dn)
            ys = swiglu(hs, ffn_w_gu, ffn_w_dn)
        else:
            yp = moe_ffn(hp, moe_w_router, moe_w_gu, moe_w_dn)
            ys = moe_ffn(hs, moe_w_router, moe_w_gu, moe_w_dn)
        xp = xp + fp_g * rmsnorm(yp, norm_g[i, 3])
        xs = xs + fs_g * rmsnorm(ys, norm_g[i, 3])
    return (xp, xs, k_prompt, v_prompt, k_sample, v_sample, ssm_prompt, conv_prompt, ssm_sample, conv_sample)
```

```python
import functools

import jax
import jax.numpy as jnp
from jax import lax
from jax.experimental import pallas as pl
from jax.experimental.pallas import tpu as pltpu

F32 = jnp.float32
BF16 = jnp.bfloat16
EPS = 1e-6
LANES = 128
HEAD_DIM = 64
PAIR = 2 * HEAD_DIM
VMEM_LIMIT = 56 * 1024 * 1024

NT = (((1,), (1,)), ((), ()))
TN = (((0,), (0,)), ((), ()))


def _params(*sem):
    return pltpu.CompilerParams(dimension_semantics=sem, vmem_limit_bytes=VMEM_LIMIT)


def _silu(x):
    return x * jax.nn.sigmoid(x)


def _softplus(z):
    return jnp.maximum(z, 0.0) + jnp.log(1.0 + jnp.exp(-jnp.abs(z)))


def _rms(x, g):
    return x * lax.rsqrt(jnp.mean(x * x, axis=-1, keepdims=True) + EPS) * g


def _normmod(x, g, shift, scale):
    return _rms(x, g) * (1.0 + scale) + shift


def _dot(a, b):
    return jnp.dot(a, b, preferred_element_type=F32)


def _dot_hi(a, b, dims=None):
    if dims is None:
        return jnp.dot(a, b, preferred_element_type=F32, precision=lax.Precision.HIGHEST)
    return lax.dot_general(a, b, dims, preferred_element_type=F32, precision=lax.Precision.HIGHEST)


def _split_dot(a, b_bf16):
    hi = a.astype(BF16)
    lo = (a - hi.astype(F32)).astype(BF16)
    return _dot(hi, b_bf16) + _dot(lo, b_bf16)


def _mod_spec(mod, d, col, bpg):
    r = mod.shape[1]
    return pl.BlockSpec((None, r, d), lambda i, *_: (i // bpg, 0, col))


def _g_spec(d, idx):
    return pl.BlockSpec((None, 1, d), lambda i, *_: (idx, 0, 0))


def _adaln_kernel(c_ref, w_ref, b_ref, o_ref):
    s = _silu(c_ref[...]).astype(BF16)
    o_ref[...] = _dot(s, w_ref[...].astype(BF16)) + b_ref[...]


def adaln_all(c, w_ada, b_ada, tn=1024):
    depth, d, n = w_ada.shape
    m = c.shape[0]
    return pl.pallas_call(
        _adaln_kernel,
        grid=(depth, n // tn),
        in_specs=[pl.BlockSpec((m, d), lambda l, j: (0, 0)),
                  pl.BlockSpec((None, d, tn), lambda l, j: (l, 0, j)),
                  pl.BlockSpec((None, 1, tn), lambda l, j: (l, 0, j))],
        out_specs=pl.BlockSpec((None, m, tn), lambda l, j: (l, 0, j)),
        out_shape=jax.ShapeDtypeStruct((depth, m, n), F32),
        compiler_params=_params("arbitrary", "arbitrary"),
        name="adaln",
    )(c, w_ada, b_ada.reshape(depth, 1, n))


def _nm_matmul_kernel(x_ref, g_ref, sh_ref, sc_ref, w_ref, *rest, n_extra):
    extra_w = rest[:n_extra]
    o_ref = rest[n_extra]
    extra_o = rest[n_extra + 1:2 * n_extra + 1]
    h_scr = rest[-1]

    @pl.when(pl.program_id(1) == 0)
    def _():
        h_scr[...] = _normmod(x_ref[...], g_ref[...], sh_ref[...], sc_ref[...]).astype(BF16)
        for w2, o2 in zip(extra_w, extra_o):
            o2[...] = _dot(h_scr[...], w2[...])

    o_ref[...] = _dot(h_scr[...], w_ref[...])


def nm_matmul(x, g_all, g_idx, mod, sh_col, sc_col, w, tm, tn, extra_w=()):
    m, d = x.shape
    n = w.shape[1]
    bpg = (m // tm) // mod.shape[0]
    in_specs = [pl.BlockSpec((tm, d), lambda i, j: (i, 0)),
                _g_spec(d, g_idx), _mod_spec(mod, d, sh_col, bpg), _mod_spec(mod, d, sc_col, bpg),
                pl.BlockSpec((d, tn), lambda i, j: (0, j))]
    out_specs = [pl.BlockSpec((tm, tn), lambda i, j: (i, j))]
    out_shape = [jax.ShapeDtypeStruct((m, n), F32)]
    for w2 in extra_w:
        in_specs.append(pl.BlockSpec(w2.shape, lambda i, j: (0, 0)))
        out_specs.append(pl.BlockSpec((tm, w2.shape[1]), lambda i, j: (i, 0)))
        out_shape.append(jax.ShapeDtypeStruct((m, w2.shape[1]), F32))
    outs = pl.pallas_call(
        functools.partial(_nm_matmul_kernel, n_extra=len(extra_w)),
        grid=(m // tm, n // tn),
        in_specs=in_specs, out_specs=out_specs, out_shape=out_shape,
        scratch_shapes=[pltpu.VMEM((tm, d), BF16)],
        compiler_params=_params("arbitrary", "arbitrary"),
        name="nm_matmul",
    )(x, g_all, mod, mod, w, *extra_w)
    return outs if extra_w else outs[0]


def _qkv_kernel(x_ref, g_ref, sh_ref, sc_ref, wq_ref, wkv_ref, qp_ref, kt_ref, vt_ref, kvb_ref, h_scr, *, nq, tn):
    j = pl.program_id(1)

    @pl.when(j == 0)
    def _():
        h_scr[...] = _normmod(x_ref[...], g_ref[...], sh_ref[...], sc_ref[...]).astype(BF16)

    @pl.when(j < nq)
    def _():
        r = _dot(h_scr[...], wq_ref[...])
        for t in range(tn // PAIR):
            qp_ref[t] = r[:, t * PAIR:(t + 1) * PAIR].astype(BF16)

    @pl.when(j >= nq)
    def _():
        rt = lax.dot_general(wkv_ref[...], h_scr[...], NT, preferred_element_type=F32)
        for t in range(tn // PAIR):
            kvb_ref[t] = rt[t * PAIR:(t + 1) * PAIR, :].astype(BF16)

        @pl.when(j < 2 * nq)
        def _():
            for t in range(tn // HEAD_DIM):
                kt_ref[t] = rt[t * HEAD_DIM:(t + 1) * HEAD_DIM, :]

        @pl.when(j >= 2 * nq)
        def _():
            for t in range(tn // HEAD_DIM):
                vt_ref[t] = rt[t * HEAD_DIM:(t + 1) * HEAD_DIM, :]


def qkv_prompt(x, g_all, g_idx, mod, sh_col, sc_col, w_q, w_kv_t, batch, tm, tn=256):
    m, d = x.shape
    s = m // batch
    bpg = s // tm
    nq = d // tn
    nh = d // HEAD_DIM
    hpt = tn // HEAD_DIM
    ppt = tn // PAIR
    return pl.pallas_call(
        functools.partial(_qkv_kernel, nq=nq, tn=tn),
        grid=(m // tm, 3 * nq),
        in_specs=[pl.BlockSpec((tm, d), lambda i, j: (i, 0)),
                  _g_spec(d, g_idx), _mod_spec(mod, d, sh_col, bpg), _mod_spec(mod, d, sc_col, bpg),
                  pl.BlockSpec((d, tn), lambda i, j: (0, jnp.minimum(j, nq - 1))),
                  pl.BlockSpec((tn, d), lambda i, j: (jnp.maximum(j - nq, 0), 0))],
        out_specs=[pl.BlockSpec((None, ppt, tm, PAIR), lambda i, j: (i // bpg, jnp.minimum(j, nq - 1), i % bpg, 0)),
                   pl.BlockSpec((None, hpt, HEAD_DIM, tm),
                                lambda i, j: (i // bpg, jnp.clip(j - nq, 0, nq - 1), 0, i % bpg)),
                   pl.BlockSpec((None, hpt, HEAD_DIM, tm),
                                lambda i, j: (i // bpg, jnp.maximum(j - 2 * nq, 0), 0, i % bpg)),
                   pl.BlockSpec((None, None, ppt, PAIR, tm),
                                lambda i, j: (i // bpg, jnp.maximum(j - nq, 0) // nq, jnp.maximum(j - nq, 0) % nq,
                                              0, i % bpg))],
        out_shape=[jax.ShapeDtypeStruct((batch, d // PAIR, s, PAIR), BF16),
                   jax.ShapeDtypeStruct((batch, nh, HEAD_DIM, s), F32),
                   jax.ShapeDtypeStruct((batch, nh, HEAD_DIM, s), F32),
                   jax.ShapeDtypeStruct((batch, 2, d // PAIR, PAIR, s), BF16)],
        scratch_shapes=[pltpu.VMEM((tm, d), BF16)],
        compiler_params=_params("arbitrary", "arbitrary"),
        name="qkv_prompt",
    )(x, g_all, mod, mod, w_q, w_kv_t)


def _sb_prompt_kernel(qi_ref, kj_ref, bias_ref, q_ref, kt_ref, vt_ref, o_ref, acc, tail, *, t):
    s = pl.program_id(1)
    i = qi_ref[s]
    j = kj_ref[s]
    first = j == i
    npairs = q_ref.shape[0]

    @pl.when(first)
    def _():
        acc[...] = jnp.zeros_like(acc)
        tail[...] = jnp.zeros_like(tail)

    row = lax.broadcasted_iota(jnp.int32, (t, t), 0)
    col = lax.broadcasted_iota(jnp.int32, (t, t), 1)
    valid = col < row + jnp.where(first, 0, t)
    tri = (row > col).astype(BF16)
    low = lax.broadcasted_iota(jnp.int32, (1, PAIR), 1) < HEAD_DIM
    rlow = lax.broadcasted_iota(jnp.int32, (PAIR, 1), 0) < HEAD_DIM

    def pair_body(hp, carry):
        q2 = q_ref[hp]
        k2 = kt_ref[hp]
        v2 = vt_ref[hp]
        o_pair = jnp.zeros((t, PAIR), F32)
        for hh in range(2):
            sel = low if hh == 0 else jnp.logical_not(low)
            rsel = rlow if hh == 0 else jnp.logical_not(rlow)
            h = 2 * hp + hh
            qh = jnp.where(sel, q2, jnp.zeros_like(q2))
            z = _dot(qh, k2) + bias_ref[h]
            sp_all = _softplus(z)
            sp = jnp.where(valid, sp_all, 0.0)
            later = _dot(sp.astype(BF16), tri)
            tl = tail[h]
            tl_w = jnp.concatenate([tl] * (t // LANES), axis=1)
            w = jnp.where(valid, jnp.exp(z - sp_all - later - tl_w), 0.0)
            vh = jnp.where(rsel, v2, jnp.zeros_like(v2))
            o_pair = o_pair + lax.dot_general(w.astype(BF16), vh, NT, preferred_element_type=F32)
            tail[h] = tl + jnp.sum(sp, axis=1, keepdims=True)
        acc[hp] = acc[hp] + o_pair
        return carry

    lax.fori_loop(0, npairs, pair_body, 0)

    @pl.when(j == 0)
    def _():
        o_ref[...] = acc[...].astype(BF16)


def sb_attention_prompt(q_pairs, kv_t, bias, t=256):
    batch, npairs, s, _ = q_pairs.shape
    nq = s // t
    qi, kj = [], []
    for i in range(nq):
        for j in range(i, -1, -1):
            qi.append(i)
            kj.append(j)
    qi = jnp.asarray(qi, jnp.int32)
    kj = jnp.asarray(kj, jnp.int32)
    blk = (None, npairs, t, PAIR)
    blk_t = (None, None, npairs, PAIR, t)
    grid_spec = pltpu.PrefetchScalarGridSpec(
        num_scalar_prefetch=2,
        grid=(batch, qi.shape[0]),
        in_specs=[pl.BlockSpec(memory_space=pltpu.SMEM),
                  pl.BlockSpec(blk, lambda b, s_, qi_, kj_: (b, 0, qi_[s_], 0)),
                  pl.BlockSpec(blk_t, lambda b, s_, qi_, kj_: (b, 0, 0, 0, kj_[s_])),
                  pl.BlockSpec(blk_t, lambda b, s_, qi_, kj_: (b, 1, 0, 0, kj_[s_]))],
        out_specs=pl.BlockSpec(blk, lambda b, s_, qi_, kj_: (b, 0, qi_[s_], 0)),
        scratch_shapes=[pltpu.VMEM((npairs, t, PAIR), F32), pltpu.VMEM((2 * npairs, t, LANES), F32)],
    )
    return pl.pallas_call(
        functools.partial(_sb_prompt_kernel, t=t),
        grid_spec=grid_spec,
        out_shape=jax.ShapeDtypeStruct((batch, npairs, s, PAIR), BF16),
        compiler_params=_params("arbitrary", "arbitrary"),
        name="sb_prompt",
    )(qi, kj, bias, q_pairs, kv_t, kv_t)


def _sb_decode_kernel(pt_ref, bias_ref, qc_ref, knc_ref, vnc_ref, *rest, pages, past_len):
    k_refs = rest[:pages]
    v_refs = rest[pages:2 * pages]
    o_ref, acc, tail = rest[2 * pages:]
    g = pl.program_id(1)
    nh, hd, page = acc.shape

    @pl.when(g == 0)
    def _():
        lane0 = lax.broadcasted_iota(jnp.int32, (1, page), 1) == 0
        pos = past_len + lax.broadcasted_iota(jnp.int32, (1, page), 1) * 0
        vis = pos < pos

        def init_head(h, carry):
            z = jnp.sum(qc_ref[h] * knc_ref[h], axis=0, keepdims=True) + bias_ref[h]
            sp = jnp.where(vis, _softplus(z), 0.0)
            w = jnp.where(vis & lane0, jnp.exp(z - _softplus(z)), 0.0)
            acc[h] = w * vnc_ref[h]
            tail[h] = jnp.broadcast_to(sp, (pages, page))
            return carry

        lax.fori_loop(0, nh, init_head, 0)

    r_ = lax.broadcasted_iota(jnp.int32, (page, page), 0)
    c_ = lax.broadcasted_iota(jnp.int32, (page, page), 1)
    tri = (r_ > c_).astype(BF16)
    slot = lax.broadcasted_iota(jnp.int32, (pages, page), 0)

    def head_body(h, carry):
        qh = qc_ref[h]
        rows = [jnp.sum(k_refs[p][h] * qh, axis=0, keepdims=True) for p in range(pages)]
        z = jnp.concatenate(rows, axis=0) + bias_ref[h]
        sp = _softplus(z)
        within = _split_dot(sp, tri)
        rs = jnp.broadcast_to(jnp.sum(sp, axis=1, keepdims=True), (pages, page))
        run = rs
        sft = 1
        while sft < pages:
            run = run + jnp.where(slot >= sft, pltpu.roll(run, sft, 0), 0.0)
            sft *= 2
        tl = tail[h]
        w = jnp.exp(z - sp - within - (run - rs) - tl)
        tail[h] = tl + jnp.broadcast_to(run[pages - 1:pages, :], (pages, page))
        a = acc[h]
        for p in range(pages):
            a = a + w[p:p + 1, :] * v_refs[p][h]
        acc[h] = a
        return carry

    lax.fori_loop(0, nh, head_body, 0)

    @pl.when(g == pl.num_programs(1) - 1)
    def _():
        o_ref[...] = jnp.sum(acc[...], axis=2)


def sb_attention_decode(q, k_new, v_new, bias, cache_k, cache_v, page_table, pages=8):
    b, d = q.shape
    n_phys, page, nh, hd = cache_k.shape
    n_pages = page_table.shape[1]
    assert page == LANES and pages == 8 and n_pages % pages == 0
    ck = jnp.transpose(cache_k, (0, 2, 3, 1))
    cv = jnp.transpose(cache_v, (0, 2, 3, 1))
    blk = (None, nh, hd, page)
    seq = pl.BlockSpec(blk, lambda i, g, pt: (i, 0, 0, 0))

    def page_spec(p):
        return pl.BlockSpec(blk, lambda i, g, pt: (pt[i, n_pages - 1 - (g * pages + p)], 0, 0, 0))

    def cols(a):
        return jnp.broadcast_to(a.reshape(b, nh, hd, 1), (b, nh, hd, page))

    grid_spec = pltpu.PrefetchScalarGridSpec(
        num_scalar_prefetch=1,
        grid=(b, n_pages // pages),
        in_specs=[pl.BlockSpec(memory_space=pltpu.SMEM), seq, seq, seq]
        + [page_spec(p) for p in range(pages)] * 2,
        out_specs=pl.BlockSpec((None, nh, hd), lambda i, g, pt: (i, 0, 0)),
        scratch_shapes=[pltpu.VMEM((nh, hd, page), F32), pltpu.VMEM((nh, pages, page), F32)],
    )
    out = pl.pallas_call(
        functools.partial(_sb_decode_kernel, pages=pages, past_len=n_pages * page),
        grid_spec=grid_spec,
        out_shape=jax.ShapeDtypeStruct((b, nh, hd), F32),
        compiler_params=_params("arbitrary", "arbitrary"),
        name="sb_decode",
    )(page_table, bias.astype(F32), cols(q), cols(k_new), cols(v_new), *([ck] * pages), *([cv] * pages))
    return out.reshape(b, d)


def _proj_post_kernel(a_ref, w_ref, x_ref, g_ref, gate_ref, o_ref, *, paired):
    if paired:
        a = jnp.concatenate([a_ref[t] for t in range(a_ref.shape[0])], axis=1)
    else:
        a = a_ref[...]
    y = _dot(a.astype(BF16), w_ref[...])
    o_ref[...] = x_ref[...] + gate_ref[...] * _rms(y, g_ref[...])


def proj_post(a, w, x, g_all, g_idx, mod, gate_col, tm):
    m, d = x.shape
    paired = a.ndim == 4
    bpg = (m // tm) // mod.shape[0]
    if paired:
        spb = a.shape[2] // tm
        a_spec = pl.BlockSpec((None, a.shape[1], tm, PAIR), lambda i: (i // spb, 0, i % spb, 0))
    else:
        a_spec = pl.BlockSpec((tm, a.shape[1]), lambda i: (i, 0))
    return pl.pallas_call(
        functools.partial(_proj_post_kernel, paired=paired),
        grid=(m // tm,),
        in_specs=[a_spec, pl.BlockSpec(w.shape, lambda i: (0, 0)), pl.BlockSpec((tm, d), lambda i: (i, 0)),
                  _g_spec(d, g_idx), _mod_spec(mod, d, gate_col, bpg)],
        out_specs=pl.BlockSpec((tm, d), lambda i: (i, 0)),
        out_shape=jax.ShapeDtypeStruct((m, d), F32),
        compiler_params=_params("arbitrary"),
        name="proj_post",
    )(a, w, x, g_all, mod)


def _ffn_kernel(x_ref, g_ref, sh_ref, sc_ref, wg_ref, wu_ref, wd_ref, gp_ref, gate_ref, o_ref, h_scr, acc):
    f = pl.program_id(1)

    @pl.when(f == 0)
    def _():
        h_scr[...] = _normmod(x_ref[...], g_ref[...], sh_ref[...], sc_ref[...]).astype(BF16)
        acc[...] = jnp.zeros_like(acc)

    h = h_scr[...]
    act = _silu(_dot(h, wg_ref[...])) * _dot(h, wu_ref[...])
    acc[...] += _dot(act.astype(BF16), wd_ref[...])

    @pl.when(f == pl.num_programs(1) - 1)
    def _():
        o_ref[...] = x_ref[...] + gate_ref[...] * _rms(acc[...], gp_ref[...])


def ffn_sublayer(x, g_all, g_pre, g_post, mod, sh_col, sc_col, gate_col, w_gu, w_dn, tm, fc):
    m, d = x.shape
    dff = w_dn.shape[0]
    nf = dff // fc
    bpg = (m // tm) // mod.shape[0]
    return pl.pallas_call(
        _ffn_kernel,
        grid=(m // tm, nf),
        in_specs=[pl.BlockSpec((tm, d), lambda i, f: (i, 0)),
                  _g_spec(d, g_pre), _mod_spec(mod, d, sh_col, bpg), _mod_spec(mod, d, sc_col, bpg),
                  pl.BlockSpec((d, fc), lambda i, f: (0, f)),
                  pl.BlockSpec((d, fc), lambda i, f: (0, nf + f)),
                  pl.BlockSpec((fc, d), lambda i, f: (f, 0)),
                  _g_spec(d, g_post), _mod_spec(mod, d, gate_col, bpg)],
        out_specs=pl.BlockSpec((tm, d), lambda i, f: (i, 0)),
        out_shape=jax.ShapeDtypeStruct((m, d), F32),
        scratch_shapes=[pltpu.VMEM((tm, d), BF16), pltpu.VMEM((tm, d), F32)],
        compiler_params=_params("arbitrary", "arbitrary"),
        name="ffn",
    )(x, g_all, mod, mod, w_gu, w_gu, w_dn, g_all, mod)


def _expand_heads(a, e_bf16):
    return _split_dot(a, e_bf16)


def _ssd_prompt_kernel(z_ref, x_ref, bc_ref, dt_ref, cwx_ref, cwbc_ref, cbx_ref, cbbc_ref, dtb_ref, a_ref,
                       dsk_ref, ng_ref, e_ref, y_ref, st_ref, conv_ref, xs_x, xs_bc, y_scr,
                       *, chunk, groups, dstate):
    c = pl.program_id(1)
    last = c == pl.num_programs(1) - 1
    L = chunk
    d_inner = x_ref.shape[1]
    npairs = d_inner // PAIR
    ppg = npairs // groups
    kw = cwx_ref.shape[0]

    @pl.when(c == 0)
    def _():
        xs_x[0:8, :] = jnp.zeros((8, xs_x.shape[1]), F32)
        xs_bc[0:8, :] = jnp.zeros((8, xs_bc.shape[1]), F32)
        st_ref[...] = jnp.zeros_like(st_ref)

    def conv_act(cur_ref, xs, cw_ref, cb_ref):
        xs[8:8 + L, :] = cur_ref[...]
        out = cb_ref[...] + cw_ref[kw - 1:kw, :] * cur_ref[...]
        for k in range(kw - 1):
            sft = kw - 1 - k
            out = out + cw_ref[k:k + 1, :] * xs[8 - sft:8 - sft + L, :]
        return _silu(out)

    x_act = conv_act(x_ref, xs_x, cwx_ref, cbx_ref)
    bc_act = conv_act(bc_ref, xs_bc, cwbc_ref, cbbc_ref)

    @pl.when(last)
    def _():
        conv_ref[:, 0:d_inner] = xs_x[8 + L - (kw - 1):8 + L, :]
        conv_ref[:, d_inner:] = xs_bc[8 + L - (kw - 1):8 + L, :]

    xs_x[0:8, :] = xs_x[L:L + 8, :]
    xs_bc[0:8, :] = xs_bc[L:L + 8, :]

    dt = _softplus(dt_ref[...] + dtb_ref[...])
    dta = dt * a_ref[...]
    r_ = lax.broadcasted_iota(jnp.int32, (L, L), 0)
    c_ = lax.broadcasted_iota(jnp.int32, (L, L), 1)
    causal = r_ >= c_
    a_cs = _dot_hi(causal.astype(F32), dta)
    a_cs_t = a_cs.T
    a_end = a_cs[L - 1:L, :]
    e = e_ref[...]
    dt_x = _expand_heads(dt, e)
    grow_x = _expand_heads(jnp.exp(a_cs), e)
    dend_x = _expand_heads(jnp.exp(a_end - a_cs), e)
    cdec = jnp.exp(a_end)

    low = lax.broadcasted_iota(jnp.int32, (1, PAIR), 1) < HEAD_DIM
    rlow = lax.broadcasted_iota(jnp.int32, (PAIR, 1), 0) < HEAD_DIM
    gw = groups * dstate
    for g in range(groups):
        bg = bc_act[:, g * dstate:(g + 1) * dstate].astype(BF16)
        cg = bc_act[:, gw + g * dstate:gw + (g + 1) * dstate].astype(BF16)
        cb = lax.dot_general(cg, bg, NT, preferred_element_type=F32)
        for pp in range(ppg):
            hp = g * ppg + pp
            sl = slice(hp * PAIR, (hp + 1) * PAIR)
            xdt = x_act[:, sl] * dt_x[:, sl]
            xdt_b = xdt.astype(BF16)
            y = jnp.zeros((L, PAIR), F32)
            for hh in range(2):
                h = 2 * hp + hh
                seg = a_cs[:, h:h + 1] - a_cs_t[h:h + 1, :]
                dec = jnp.where(causal, jnp.exp(jnp.where(causal, seg, 0.0)), 0.0)
                sel = low if hh == 0 else jnp.logical_not(low)
                y = y + _dot((cb * dec).astype(BF16), jnp.where(sel, xdt_b, jnp.zeros_like(xdt_b)))
            prev = st_ref[hp]
            y = y + lax.dot_general(cg, prev.astype(BF16), NT, preferred_element_type=F32) * grow_x[:, sl]
            y_scr[:, sl] = y
            xw = (xdt * dend_x[:, sl]).astype(BF16)
            s_c = lax.dot_general(xw, bg, TN, preferred_element_type=F32)
            cd = jnp.where(rlow, cdec[:, 2 * hp:2 * hp + 1], cdec[:, 2 * hp + 1:2 * hp + 2])
            st_ref[hp] = prev * cd + s_c

    y = y_scr[...] + x_act * dsk_ref[...]
    y = y * _silu(z_ref[...])
    gsz = d_inner // groups
    for g in range(groups):
        sl = slice(g * gsz, (g + 1) * gsz)
        y_ref[:, sl] = _rms(y[:, sl], ng_ref[:, sl]).astype(BF16)


def _head_expander(nheads):
    lane = jnp.arange(nheads * HEAD_DIM)[None, :] // HEAD_DIM
    return (lane == jnp.arange(LANES)[:, None]).astype(BF16)


def ssd_prompt(zx, dt_raw, conv_w, conv_b, dt_bias, a_neg, d_skip, norm_g, batch, chunk, groups, dstate):
    m = zx.shape[0]
    d_inner = d_skip.shape[0] * HEAD_DIM
    nheads = d_skip.shape[0]
    bcw = 2 * groups * dstate
    s = m // batch
    nc = s // chunk
    kw = conv_w.shape[0]
    npairs = d_inner // PAIR
    pad = LANES - nheads
    dtb = jnp.pad(dt_bias.astype(F32), (0, pad)).reshape(1, LANES)
    a_p = jnp.pad(a_neg, (0, pad)).reshape(1, LANES)
    dsk = jnp.repeat(d_skip.astype(F32), HEAD_DIM).reshape(1, d_inner)
    cb2 = conv_b.reshape(1, -1)
    rb = lambda b, c: b * nc + c
    const = lambda shape, j=0: pl.BlockSpec(shape, lambda b, c: (0, j))
    return pl.pallas_call(
        functools.partial(_ssd_prompt_kernel, chunk=chunk, groups=groups, dstate=dstate),
        grid=(batch, nc),
        in_specs=[pl.BlockSpec((chunk, d_inner), lambda b, c: (rb(b, c), 0)),
                  pl.BlockSpec((chunk, d_inner), lambda b, c: (rb(b, c), 1)),
                  pl.BlockSpec((chunk, bcw), lambda b, c: (rb(b, c), 2 * d_inner // bcw)),
                  pl.BlockSpec((chunk, LANES), lambda b, c: (rb(b, c), 0)),
                  const((kw, d_inner)), const((kw, bcw), d_inner // bcw),
                  const((1, d_inner)), const((1, bcw), d_inner // bcw),
                  const((1, LANES)), const((1, LANES)), const((1, d_inner)), const((1, d_inner)),
                  const((LANES, d_inner))],
        out_specs=[pl.BlockSpec((chunk, d_inner), lambda b, c: (rb(b, c), 0)),
                   pl.BlockSpec((None, npairs, PAIR, dstate), lambda b, c: (b, 0, 0, 0)),
                   pl.BlockSpec((None, kw - 1, d_inner + bcw), lambda b, c: (b, 0, 0))],
        out_shape=[jax.ShapeDtypeStruct((m, d_inner), BF16),
                   jax.ShapeDtypeStruct((batch, npairs, PAIR, dstate), F32),
                   jax.ShapeDtypeStruct((batch, kw - 1, d_inner + bcw), F32)],
        scratch_shapes=[pltpu.VMEM((chunk + 8, d_inner), F32), pltpu.VMEM((chunk + 8, bcw), F32),
                        pltpu.VMEM((chunk, d_inner), F32)],
        compiler_params=_params("arbitrary", "arbitrary"),
        name="ssd_prompt",
    )(zx, zx, zx, dt_raw, conv_w, conv_w, cb2, cb2, dtb, a_p, dsk, norm_g.reshape(1, d_inner),
      _head_expander(nheads))


def _ssd_step_kernel(z_ref, x_ref, bc_ref, dt_ref, sx_ref, sbc_ref, st_ref, cwx_ref, cwbc_ref, cbx_ref, cbbc_ref,
                     dtb_ref, a_ref, dsk_ref, ng_ref, y_ref, sto_ref, conv_ref, y_scr, *, groups, dstate):
    d_inner = x_ref.shape[1]
    npairs = d_inner // PAIR
    ppg = npairs // groups
    kw = cwx_ref.shape[0]

    def conv_act(new_ref, s_ref, cw_ref, cb_ref):
        out = cb_ref[...] + cw_ref[kw - 1:kw, :] * new_ref[...]
        for k in range(kw - 1):
            out = out + cw_ref[k:k + 1, :] * s_ref[k:k + 1, :]
        return _silu(out)

    x_act = conv_act(x_ref, sx_ref, cwx_ref, cbx_ref)
    bc_act = conv_act(bc_ref, sbc_ref, cwbc_ref, cbbc_ref)
    conv_ref[0:kw - 2, 0:d_inner] = sx_ref[1:kw - 1, :]
    conv_ref[0:kw - 2, d_inner:] = sbc_ref[1:kw - 1, :]
    conv_ref[kw - 2:kw - 1, 0:d_inner] = x_ref[...]
    conv_ref[kw - 2:kw - 1, d_inner:] = bc_ref[...]

    dt = _softplus(dt_ref[...] + dtb_ref[...])
    decay = jnp.exp(dt * a_ref[...])
    rlow = lax.broadcasted_iota(jnp.int32, (PAIR, 1), 0) < HEAD_DIM
    low = lax.broadcasted_iota(jnp.int32, (1, PAIR), 1) < HEAD_DIM
    eye = lax.broadcasted_iota(jnp.int32, (PAIR, PAIR), 0) == lax.broadcasted_iota(jnp.int32, (PAIR, PAIR), 1)
    gw = groups * dstate
    for hp in range(npairs):
        g = hp // ppg
        sl = slice(hp * PAIR, (hp + 1) * PAIR)
        bg = jnp.broadcast_to(bc_act[:, g * dstate:(g + 1) * dstate], (PAIR, dstate))
        cg = jnp.broadcast_to(bc_act[:, gw + g * dstate:gw + (g + 1) * dstate], (8, dstate))
        dt_p = jnp.where(low, dt[:, 2 * hp:2 * hp + 1], dt[:, 2 * hp + 1:2 * hp + 2])
        xdt = x_act[:, sl] * dt_p
        outer = _dot_hi(jnp.where(eye, jnp.broadcast_to(xdt, (PAIR, PAIR)), 0.0), bg)
        dc = jnp.where(rlow, decay[:, 2 * hp:2 * hp + 1], decay[:, 2 * hp + 1:2 * hp + 2])
        new = st_ref[hp] * dc + outer
        sto_ref[hp] = new
        y_scr[:, sl] = _dot_hi(cg, new, NT)[0:1, :]

    y = y_scr[...] + x_act * dsk_ref[...]
    y = y * _silu(z_ref[...])
    gsz = d_inner // groups
    for g in range(groups):
        sl = slice(g * gsz, (g + 1) * gsz)
        y_ref[:, sl] = _rms(y[:, sl], ng_ref[:, sl]).astype(BF16)


def ssd_step(zx, dt_raw, state_conv, state_ssm, conv_w, conv_b, dt_bias, a_neg, d_skip, norm_g, groups, dstate):
    b = zx.shape[0]
    nheads = d_skip.shape[0]
    d_inner = nheads * HEAD_DIM
    bcw = 2 * groups * dstate
    kw = conv_w.shape[0]
    npairs = d_inner // PAIR
    pad = LANES - nheads
    dtb = jnp.pad(dt_bias.astype(F32), (0, pad)).reshape(1, LANES)
    a_p = jnp.pad(a_neg, (0, pad)).reshape(1, LANES)
    dsk = jnp.repeat(d_skip.astype(F32), HEAD_DIM).reshape(1, d_inner)
    cb2 = conv_b.reshape(1, -1)
    zx3 = zx.reshape(b, 1, -1)
    st = state_ssm.reshape(b, npairs, PAIR, dstate)
    const = lambda shape, j=0: pl.BlockSpec(shape, lambda i: (0, j))
    y, st_new, conv_new = pl.pallas_call(
        functools.partial(_ssd_step_kernel, groups=groups, dstate=dstate),
        grid=(b,),
        in_specs=[pl.BlockSpec((None, 1, d_inner), lambda i: (i, 0, 0)),
                  pl.BlockSpec((None, 1, d_inner), lambda i: (i, 0, 1)),
                  pl.BlockSpec((None, 1, bcw), lambda i: (i, 0, 2 * d_inner // bcw)),
                  pl.BlockSpec((None, 1, LANES), lambda i: (i, 0, 0)),
                  pl.BlockSpec((None, kw - 1, d_inner), lambda i: (i, 0, 0)),
                  pl.BlockSpec((None, kw - 1, bcw), lambda i: (i, 0, d_inner // bcw)),
                  pl.BlockSpec((None, npairs, PAIR, dstate), lambda i: (i, 0, 0, 0)),
                  const((kw, d_inner)), const((kw, bcw), d_inner // bcw),
                  const((1, d_inner)), const((1, bcw), d_inner // bcw),
                  const((1, LANES)), const((1, LANES)), const((1, d_inner)), const((1, d_inner))],
        out_specs=[pl.BlockSpec((None, 1, d_inner), lambda i: (i, 0, 0)),
                   pl.BlockSpec((None, npairs, PAIR, dstate), lambda i: (i, 0, 0, 0)),
                   pl.BlockSpec((None, kw - 1, d_inner + bcw), lambda i: (i, 0, 0))],
        out_shape=[jax.ShapeDtypeStruct((b, 1, d_inner), BF16),
                   jax.ShapeDtypeStruct((b, npairs, PAIR, dstate), F32),
                   jax.ShapeDtypeStruct((b, kw - 1, d_inner + bcw), F32)],
        scratch_shapes=[pltpu.VMEM((1, d_inner), F32)],
        compiler_params=_params("arbitrary"),
        name="ssd_step",
    )(zx3, zx3, zx3, dt_raw.reshape(b, 1, LANES), state_conv, state_conv, st, conv_w, conv_w, cb2, cb2,
      dtb, a_p, dsk, norm_g.reshape(1, d_inner))
    return y.reshape(b, d_inner), st_new, conv_new


def _router_kernel(x_ref, g_ref, sh_ref, sc_ref, wr_ref, h_ref, comb_ref, *, n_experts):
    h = _normmod(x_ref[...], g_ref[...], sh_ref[...], sc_ref[...])
    h_ref[...] = h.astype(BF16)
    logits = _dot_hi(h, wr_ref[...])
    lane = lax.broadcasted_iota(jnp.int32, logits.shape, 1).astype(F32)
    neg = jnp.float32(-jnp.inf)
    lg = jnp.where(lane < n_experts, logits, neg)
    m1 = jnp.max(lg, axis=1, keepdims=True)
    i1 = jnp.min(jnp.where(lg == m1, lane, float(LANES)), axis=1, keepdims=True)
    lg2 = jnp.where(lane == i1, neg, lg)
    m2 = jnp.max(lg2, axis=1, keepdims=True)
    i2 = jnp.min(jnp.where(lg2 == m2, lane, float(LANES)), axis=1, keepdims=True)
    e2 = jnp.exp(m2 - m1)
    g1 = 1.0 / (1.0 + e2)
    comb_ref[...] = jnp.where(lane == i1, g1, 0.0) + jnp.where(lane == i2, e2 * g1, 0.0)


def moe_router(x, g_all, g_idx, mod, sh_col, sc_col, w_router, tm):
    m, d = x.shape
    n_experts = w_router.shape[1]
    bpg = (m // tm) // mod.shape[0]
    wr = jnp.pad(w_router.astype(F32), ((0, 0), (0, LANES - n_experts)))
    return pl.pallas_call(
        functools.partial(_router_kernel, n_experts=n_experts),
        grid=(m // tm,),
        in_specs=[pl.BlockSpec((tm, d), lambda i: (i, 0)),
                  _g_spec(d, g_idx), _mod_spec(mod, d, sh_col, bpg), _mod_spec(mod, d, sc_col, bpg),
                  pl.BlockSpec((d, LANES), lambda i: (0, 0))],
        out_specs=[pl.BlockSpec((tm, d), lambda i: (i, 0)), pl.BlockSpec((tm, LANES), lambda i: (i, 0))],
        out_shape=[jax.ShapeDtypeStruct((m, d), BF16), jax.ShapeDtypeStruct((m, LANES), F32)],
        compiler_params=_params("arbitrary"),
        name="moe_router",
    )(x, g_all, mod, mod, wr)


def _moe_dense_kernel(h_ref, comb_ref, wg_ref, wu_ref, wd_ref, x_ref, gp_ref, gate_ref, o_ref, acc):
    e = pl.program_id(1)
    f = pl.program_id(2)

    @pl.when((e == 0) & (f == 0))
    def _():
        acc[...] = jnp.zeros_like(acc)

    h = h_ref[...]
    act = _silu(_dot(h, wg_ref[...])) * _dot(h, wu_ref[...])
    lane = lax.broadcasted_iota(jnp.int32, comb_ref.shape, 1)
    ce = jnp.sum(jnp.where(lane == e, comb_ref[...], 0.0), axis=1, keepdims=True)
    acc[...] += ce * _dot(act.astype(BF16), wd_ref[...])

    @pl.when((e == pl.num_programs(1) - 1) & (f == pl.num_programs(2) - 1))
    def _():
        o_ref[...] = x_ref[...] + gate_ref[...] * _rms(acc[...], gp_ref[...])


def moe_dense(h, comb, w_gu, w_dn, x, g_all, g_post, mod, gate_col, tm, fc):
    m, d = x.shape
    n_experts, dff = w_dn.shape[:2]
    nf = dff // fc
    bpg = (m // tm) // mod.shape[0]
    return pl.pallas_call(
        _moe_dense_kernel,
        grid=(m // tm, n_experts, nf),
        in_specs=[pl.BlockSpec((tm, d), lambda i, e, f: (i, 0)),
                  pl.BlockSpec((tm, LANES), lambda i, e, f: (i, 0)),
                  pl.BlockSpec((None, d, fc), lambda i, e, f: (e, 0, f)),
                  pl.BlockSpec((None, d, fc), lambda i, e, f: (e, 0, nf + f)),
                  pl.BlockSpec((None, fc, d), lambda i, e, f: (e, f, 0)),
                  pl.BlockSpec((tm, d), lambda i, e, f: (i, 0)),
                  _g_spec(d, g_post), _mod_spec(mod, d, gate_col, bpg)],
        out_specs=pl.BlockSpec((tm, d), lambda i, e, f: (i, 0)),
        out_shape=jax.ShapeDtypeStruct((m, d), F32),
        scratch_shapes=[pltpu.VMEM((tm, d), F32)],
        compiler_params=_params("arbitrary", "arbitrary", "arbitrary"),
        name="moe_dense",
    )(h, comb, w_gu, w_gu, w_dn, x, g_all, mod)


def kernel(x_prompt, x_sample, cache_k, cache_v, state_ssm, state_conv, page_table, c_prompt, c_sample, norm_g, w_ada, b_ada, sb_w_qkv, sb_w_o, sb_logit_bias, ssd_w_in, ssd_conv_w, ssd_conv_b, ssd_dt_bias, ssd_a_log, ssd_d, ssd_norm_g, ssd_w_out, ffn_w_gu, ffn_w_dn, moe_w_router, moe_w_gu, moe_w_dn):
    bp, seq, d = x_prompt.shape
    bs = x_sample.shape[0]
    nh_sb = sb_logit_bias.shape[0]
    nh_ssd = ssd_d.shape[0]
    d_inner = nh_ssd * HEAD_DIM
    dstate = state_ssm.shape[-1]
    groups = (ssd_conv_w.shape[1] - d_inner) // (2 * dstate)
    n_zx = d_inner + ssd_conv_w.shape[1]
    mp = bp * seq
    tm = min(512, seq)

    c_all = jnp.concatenate([c_prompt, c_sample], axis=0)
    c_all = jnp.pad(c_all, ((0, -c_all.shape[0] % 8), (0, 0)))
    ada = adaln_all(c_all, w_ada, b_ada)
    g_all = norm_g.reshape(-1, 1, d)

    qscale = jnp.concatenate([jnp.full((d,), HEAD_DIM ** -0.5, F32), jnp.ones((2 * d,), F32)])
    w_qkv = (sb_w_qkv * qscale).astype(BF16)
    w_o = sb_w_o.astype(BF16)
    w_gu = ffn_w_gu.astype(BF16)
    w_dn = ffn_w_dn.astype(BF16)
    w_in = ssd_w_in[:, :n_zx].astype(BF16)
    w_dt = jnp.pad(ssd_w_in[:, n_zx:], ((0, 0), (0, LANES - nh_ssd))).astype(BF16)
    w_out = ssd_w_out.astype(BF16)
    mw_gu = moe_w_gu.astype(BF16)
    mw_dn = moe_w_dn.astype(BF16)
    a_neg = -jnp.exp(ssd_a_log.astype(F32))

    xp = x_prompt.reshape(mp, d)
    xs = x_sample.reshape(bs, d)

    mod_p = ada[0, :bp].reshape(bp, 1, 6 * d)
    mod_s = ada[0, bp:bp + bs].reshape(1, bs, 6 * d)
    q_pairs, k_t, v_t, kv_t = qkv_prompt(xp, g_all, 0, mod_p, 0, 1, w_qkv, w_qkv[:, d:].T, bp, tm)
    o_p = sb_attention_prompt(q_pairs, kv_t, sb_logit_bias.astype(F32), t=min(256, seq))
    xp = proj_post(o_p, w_o, xp, g_all, 1, mod_p, 2, tm)
    xp = ffn_sublayer(xp, g_all, 2, 3, mod_p, 3, 4, 5, w_gu, w_dn, tm, w_dn.shape[0] // 2)

    qkv_s = nm_matmul(xs, g_all, 0, mod_s, 0, 1, w_qkv, bs, 512)
    q_s, k_s, v_s = qkv_s[:, :d], qkv_s[:, d:2 * d], qkv_s[:, 2 * d:]
    o_s = sb_attention_decode(q_s, k_s, v_s, sb_logit_bias, cache_k, cache_v, page_table)
    xs = proj_post(o_s, w_o, xs, g_all, 1, mod_s, 2, bs)
    xs = ffn_sublayer(xs, g_all, 2, 3, mod_s, 3, 4, 5, w_gu, w_dn, bs, w_dn.shape[0] // 2)

    mod_p = ada[1, :bp].reshape(bp, 1, 6 * d)
    mod_s = ada[1, bp:bp + bs].reshape(1, bs, 6 * d)
    zx_p, dt_p = nm_matmul(xp, g_all, 4, mod_p, 0, 1, w_in, tm, 512, extra_w=(w_dt,))
    y_p, ssm_p, conv_p = ssd_prompt(zx_p, dt_p, ssd_conv_w, ssd_conv_b, ssd_dt_bias, a_neg, ssd_d, ssd_norm_g,
                                    bp, min(128, seq), groups, dstate)
    xp = proj_post(y_p, w_out, xp, g_all, 5, mod_p, 2, tm)
    h_p, comb_p = moe_router(xp, g_all, 6, mod_p, 3, 4, moe_w_router, tm)
    xp = moe_dense(h_p, comb_p, mw_gu, mw_dn, xp, g_all, 7, mod_p, 5, tm, 512)

    zx_s, dt_s = nm_matmul(xs, g_all, 4, mod_s, 0, 1, w_in, bs, 512, extra_w=(w_dt,))
    y_s, ssm_s, conv_s = ssd_step(zx_s, dt_s, state_conv, state_ssm, ssd_conv_w, ssd_conv_b, ssd_dt_bias, a_neg,
                                  ssd_d, ssd_norm_g, groups, dstate)
    xs = proj_post(y_s, w_out, xs, g_all, 5, mod_s, 2, bs)
    h_s, comb_s = moe_router(xs, g_all, 6, mod_s, 3, 4, moe_w_router, bs)
    xs = moe_dense(h_s, comb_s, mw_gu, mw_dn, xs, g_all, 7, mod_s, 5, bs, 512)

    hd = (nh_sb, d // nh_sb)
    return (xp.reshape(bp, seq, d), xs.reshape(bs, 1, d),
            jnp.transpose(k_t, (0, 3, 1, 2)), jnp.transpose(v_t, (0, 3, 1, 2)),
            k_s.reshape(bs, 1, *hd), v_s.reshape(bs, 1, *hd),
            ssm_p.reshape(bp, nh_ssd, HEAD_DIM, dstate), conv_p,
            ssm_s.reshape(bs, nh_ssd, HEAD_DIM, dstate), conv_s)
```

```python
import functools

import jax
import jax.numpy as jnp
from jax import lax
from jax.experimental import pallas as pl
from jax.experimental.pallas import tpu as pltpu

F32 = jnp.float32
BF16 = jnp.bfloat16
EPS = 1e-6
LANES = 128
HEAD_DIM = 64
PAIR = 2 * HEAD_DIM
VMEM_LIMIT = 56 * 1024 * 1024

NT = (((1,), (1,)), ((), ()))
TN = (((0,), (0,)), ((), ()))


def _params(*sem):
    return pltpu.CompilerParams(dimension_semantics=sem, vmem_limit_bytes=VMEM_LIMIT)


def _silu(x):
    return x * jax.nn.sigmoid(x)


def _softplus(z):
    return jnp.maximum(z, 0.0) + jnp.log(1.0 + jnp.exp(-jnp.abs(z)))


def _sb_logits(z):
    mx = jnp.maximum(z, 0.0)
    mn = jnp.minimum(z, 0.0)
    lg = jnp.log2(1.0 + jnp.exp2(mn - mx))
    return mx + lg, mn - lg


def _rms(x, g):
    return x * lax.rsqrt(jnp.mean(x * x, axis=-1, keepdims=True) + EPS) * g


def _normmod(x, g, shift, scale):
    return _rms(x, g) * (1.0 + scale) + shift


def _dot(a, b):
    return jnp.dot(a, b, preferred_element_type=F32)


def _dot_hi(a, b, dims=None):
    if dims is None:
        return jnp.dot(a, b, preferred_element_type=F32, precision=lax.Precision.HIGHEST)
    return lax.dot_general(a, b, dims, preferred_element_type=F32, precision=lax.Precision.HIGHEST)


def _split_dot(a, b_bf16):
    hi = a.astype(BF16)
    lo = (a - hi.astype(F32)).astype(BF16)
    return _dot(hi, b_bf16) + _dot(lo, b_bf16)


def _mod_spec(mod, d, col, bpg):
    r = mod.shape[1]
    return pl.BlockSpec((None, r, d), lambda i, *_: (i // bpg, 0, col))


def _g_spec(d, idx):
    return pl.BlockSpec((None, 1, d), lambda i, *_: (idx, 0, 0))


def _adaln_kernel(c_ref, w_ref, b_ref, o_ref):
    s = _silu(c_ref[...]).astype(BF16)
    o_ref[...] = _dot(s, w_ref[...].astype(BF16)) + b_ref[...]


def adaln_all(c, w_ada, b_ada, tn=1024):
    depth, d, n = w_ada.shape
    m = c.shape[0]
    return pl.pallas_call(
        _adaln_kernel,
        grid=(depth, n // tn),
        in_specs=[pl.BlockSpec((m, d), lambda l, j: (0, 0)),
                  pl.BlockSpec((None, d, tn), lambda l, j: (l, 0, j)),
                  pl.BlockSpec((None, 1, tn), lambda l, j: (l, 0, j))],
        out_specs=pl.BlockSpec((None, m, tn), lambda l, j: (l, 0, j)),
        out_shape=jax.ShapeDtypeStruct((depth, m, n), F32),
        compiler_params=_params("arbitrary", "arbitrary"),
        name="adaln",
    )(c, w_ada, b_ada.reshape(depth, 1, n))


def _nm_matmul_kernel(x_ref, g_ref, sh_ref, sc_ref, w_ref, *rest, n_extra):
    extra_w = rest[:n_extra]
    o_ref = rest[n_extra]
    extra_o = rest[n_extra + 1:2 * n_extra + 1]
    h_scr = rest[-1]

    @pl.when(pl.program_id(1) == 0)
    def _():
        h_scr[...] = _normmod(x_ref[...], g_ref[...], sh_ref[...], sc_ref[...]).astype(BF16)
        for w2, o2 in zip(extra_w, extra_o):
            o2[...] = _dot(h_scr[...], w2[...])

    o_ref[...] = _dot(h_scr[...], w_ref[...])


def nm_matmul(x, g_all, g_idx, mod, sh_col, sc_col, w, tm, tn, extra_w=()):
    m, d = x.shape
    n = w.shape[1]
    bpg = (m // tm) // mod.shape[0]
    in_specs = [pl.BlockSpec((tm, d), lambda i, j: (i, 0)),
                _g_spec(d, g_idx), _mod_spec(mod, d, sh_col, bpg), _mod_spec(mod, d, sc_col, bpg),
                pl.BlockSpec((d, tn), lambda i, j: (0, j))]
    out_specs = [pl.BlockSpec((tm, tn), lambda i, j: (i, j))]
    out_shape = [jax.ShapeDtypeStruct((m, n), F32)]
    for w2 in extra_w:
        in_specs.append(pl.BlockSpec(w2.shape, lambda i, j: (0, 0)))
        out_specs.append(pl.BlockSpec((tm, w2.shape[1]), lambda i, j: (i, 0)))
        out_shape.append(jax.ShapeDtypeStruct((m, w2.shape[1]), F32))
    outs = pl.pallas_call(
        functools.partial(_nm_matmul_kernel, n_extra=len(extra_w)),
        grid=(m // tm, n // tn),
        in_specs=in_specs, out_specs=out_specs, out_shape=out_shape,
        scratch_shapes=[pltpu.VMEM((tm, d), BF16)],
        compiler_params=_params("arbitrary", "arbitrary"),
        name="nm_matmul",
    )(x, g_all, mod, mod, w, *extra_w)
    return outs if extra_w else outs[0]


def _qkv_kernel(x_ref, g_ref, sh_ref, sc_ref, wq_ref, wkv_ref, qp_ref, kt_ref, vt_ref, kvb_ref, h_scr, *, nq, tn):
    j = pl.program_id(1)

    @pl.when(j == 0)
    def _():
        h_scr[...] = _normmod(x_ref[...], g_ref[...], sh_ref[...], sc_ref[...]).astype(BF16)

    @pl.when(j < nq)
    def _():
        r = _dot(h_scr[...], wq_ref[...])
        for t in range(tn // PAIR):
            qp_ref[t] = r[:, t * PAIR:(t + 1) * PAIR].astype(BF16)

    @pl.when(j >= nq)
    def _():
        rt = lax.dot_general(wkv_ref[...], h_scr[...], NT, preferred_element_type=F32)
        for t in range(tn // PAIR):
            kvb_ref[t] = rt[t * PAIR:(t + 1) * PAIR, :].astype(BF16)

        @pl.when(j < 2 * nq)
        def _():
            for t in range(tn // HEAD_DIM):
                kt_ref[t] = rt[t * HEAD_DIM:(t + 1) * HEAD_DIM, :]

        @pl.when(j >= 2 * nq)
        def _():
            for t in range(tn // HEAD_DIM):
                vt_ref[t] = rt[t * HEAD_DIM:(t + 1) * HEAD_DIM, :]


def qkv_prompt(x, g_all, g_idx, mod, sh_col, sc_col, w_q, w_kv_t, batch, tm, tn=256):
    m, d = x.shape
    s = m // batch
    bpg = s // tm
    nq = d // tn
    nh = d // HEAD_DIM
    hpt = tn // HEAD_DIM
    ppt = tn // PAIR
    return pl.pallas_call(
        functools.partial(_qkv_kernel, nq=nq, tn=tn),
        grid=(m // tm, 3 * nq),
        in_specs=[pl.BlockSpec((tm, d), lambda i, j: (i, 0)),
                  _g_spec(d, g_idx), _mod_spec(mod, d, sh_col, bpg), _mod_spec(mod, d, sc_col, bpg),
                  pl.BlockSpec((d, tn), lambda i, j: (0, jnp.minimum(j, nq - 1))),
                  pl.BlockSpec((tn, d), lambda i, j: (jnp.maximum(j - nq, 0), 0))],
        out_specs=[pl.BlockSpec((None, ppt, tm, PAIR), lambda i, j: (i // bpg, jnp.minimum(j, nq - 1), i % bpg, 0)),
                   pl.BlockSpec((None, hpt, HEAD_DIM, tm),
                                lambda i, j: (i // bpg, jnp.clip(j - nq, 0, nq - 1), 0, i % bpg)),
                   pl.BlockSpec((None, hpt, HEAD_DIM, tm),
                                lambda i, j: (i // bpg, jnp.maximum(j - 2 * nq, 0), 0, i % bpg)),
                   pl.BlockSpec((None, None, ppt, PAIR, tm),
                                lambda i, j: (i // bpg, jnp.maximum(j - nq, 0) // nq, jnp.maximum(j - nq, 0) % nq,
                                              0, i % bpg))],
        out_shape=[jax.ShapeDtypeStruct((batch, d // PAIR, s, PAIR), BF16),
                   jax.ShapeDtypeStruct((batch, nh, HEAD_DIM, s), F32),
                   jax.ShapeDtypeStruct((batch, nh, HEAD_DIM, s), F32),
                   jax.ShapeDtypeStruct((batch, 2, d // PAIR, PAIR, s), BF16)],
        scratch_shapes=[pltpu.VMEM((tm, d), BF16)],
        compiler_params=_params("arbitrary", "arbitrary"),
        name="qkv_prompt",
    )(x, g_all, mod, mod, w_q, w_kv_t)


def _sb_prompt_kernel(qi_ref, kj_ref, bias_ref, q_ref, kt_ref, vt_ref, o_ref, acc, tail, *, t):
    s = pl.program_id(1)
    i = qi_ref[s]
    j = kj_ref[s]
    first = j == i
    npairs = q_ref.shape[0]

    @pl.when(first)
    def _():
        acc[...] = jnp.zeros_like(acc)
        tail[...] = jnp.zeros_like(tail)

    row = lax.broadcasted_iota(jnp.int32, (t, t), 0)
    col = lax.broadcasted_iota(jnp.int32, (t, t), 1)
    tri = (row > col).astype(BF16)
    low = lax.broadcasted_iota(jnp.int32, (1, PAIR), 1) < HEAD_DIM
    rlow = lax.broadcasted_iota(jnp.int32, (PAIR, 1), 0) < HEAD_DIM

    def pair_body(hp, carry, *, diagonal):
        q2 = q_ref[hp]
        k2 = kt_ref[hp]
        v2 = vt_ref[hp]
        o_pair = jnp.zeros((t, PAIR), F32)
        for hh in range(2):
            sel = low if hh == 0 else jnp.logical_not(low)
            rsel = rlow if hh == 0 else jnp.logical_not(rlow)
            h = 2 * hp + hh
            qh = jnp.where(sel, q2, jnp.zeros_like(q2))
            sp, lsig = _sb_logits(_dot(qh, k2) + bias_ref[h])
            if diagonal:
                sp = jnp.where(col < row, sp, 0.0)
            later = _dot(sp.astype(BF16), tri)
            tl = tail[h]
            w = jnp.exp2(lsig - later - jnp.concatenate([tl] * (t // LANES), axis=1))
            if diagonal:
                w = jnp.where(col < row, w, 0.0)
            vh = jnp.where(rsel, v2, jnp.zeros_like(v2))
            o_pair = o_pair + lax.dot_general(w.astype(BF16), vh, NT, preferred_element_type=F32)
            tail[h] = tl + jnp.sum(sp, axis=1, keepdims=True)
        acc[hp] = acc[hp] + o_pair
        return carry

    @pl.when(first)
    def _():
        lax.fori_loop(0, npairs, functools.partial(pair_body, diagonal=True), 0, unroll=2)

    @pl.when(jnp.logical_not(first))
    def _():
        lax.fori_loop(0, npairs, functools.partial(pair_body, diagonal=False), 0, unroll=2)

    @pl.when(j == 0)
    def _():
        o_ref[...] = acc[...].astype(BF16)


def sb_attention_prompt(q_pairs, kv_t, bias, t=256):
    batch, npairs, s, _ = q_pairs.shape
    nq = s // t
    qi, kj = [], []
    for i in range(nq):
        for j in range(i, -1, -1):
            qi.append(i)
            kj.append(j)
    qi = jnp.asarray(qi, jnp.int32)
    kj = jnp.asarray(kj, jnp.int32)
    blk = (None, npairs, t, PAIR)
    blk_t = (None, None, npairs, PAIR, t)
    grid_spec = pltpu.PrefetchScalarGridSpec(
        num_scalar_prefetch=2,
        grid=(batch, qi.shape[0]),
        in_specs=[pl.BlockSpec(memory_space=pltpu.SMEM),
                  pl.BlockSpec(blk, lambda b, s_, qi_, kj_: (b, 0, qi_[s_], 0)),
                  pl.BlockSpec(blk_t, lambda b, s_, qi_, kj_: (b, 0, 0, 0, kj_[s_])),
                  pl.BlockSpec(blk_t, lambda b, s_, qi_, kj_: (b, 1, 0, 0, kj_[s_]))],
        out_specs=pl.BlockSpec(blk, lambda b, s_, qi_, kj_: (b, 0, qi_[s_], 0)),
        scratch_shapes=[pltpu.VMEM((npairs, t, PAIR), F32), pltpu.VMEM((2 * npairs, t, LANES), F32)],
    )
    return pl.pallas_call(
        functools.partial(_sb_prompt_kernel, t=t),
        grid_spec=grid_spec,
        out_shape=jax.ShapeDtypeStruct((batch, npairs, s, PAIR), BF16),
        compiler_params=_params("arbitrary", "arbitrary"),
        name="sb_prompt",
    )(qi, kj, bias, q_pairs, kv_t, kv_t)


def _sb_decode_kernel(pt_ref, bias_ref, qc_ref, knc_ref, vnc_ref, *rest, pages, past_len):
    k_refs = rest[:pages]
    v_refs = rest[pages:2 * pages]
    o_ref, acc, tail, zw = rest[2 * pages:]
    g = pl.program_id(1)
    nh, hd, page = acc.shape

    @pl.when(g == 0)
    def _():
        lane0 = lax.broadcasted_iota(jnp.int32, (1, page), 1) == 0
        pos = past_len + lax.broadcasted_iota(jnp.int32, (1, page), 1) * 0
        vis = pos < pos

        def init_head(h, carry):
            z = jnp.sum(qc_ref[h] * knc_ref[h], axis=0, keepdims=True) + bias_ref[h]
            sp, lsig = _sb_logits(z)
            acc[h] = jnp.where(vis & lane0, jnp.exp2(lsig), 0.0) * vnc_ref[h]
            tail[h] = jnp.broadcast_to(jnp.where(vis, sp, 0.0), (pages, page))
            return carry

        lax.fori_loop(0, nh, init_head, 0)

    def logits_body(h, carry):
        qh = qc_ref[h]
        rows = [jnp.sum(k_refs[p][h] * qh, axis=0, keepdims=True) for p in range(pages)]
        zw[h] = jnp.concatenate(rows, axis=0) + bias_ref[h]
        return carry

    lax.fori_loop(0, nh, logits_body, 0, unroll=2)

    r_ = lax.broadcasted_iota(jnp.int32, (page, page), 0)
    c_ = lax.broadcasted_iota(jnp.int32, (page, page), 1)
    tri = (r_ > c_).astype(BF16)
    slot = lax.broadcasted_iota(jnp.int32, (nh * pages, page), 0) % pages
    sp, lsig = _sb_logits(zw[...].reshape(nh * pages, page))
    within = _split_dot(sp, tri)
    rs = jnp.broadcast_to(jnp.sum(sp, axis=1, keepdims=True), sp.shape)
    run = rs
    sft = 1
    while sft < pages:
        run = run + jnp.where(slot >= sft, pltpu.roll(run, sft, 0), 0.0)
        sft *= 2
    tl = tail[...].reshape(nh * pages, page)
    zw[...] = jnp.exp2(lsig - within - (run - rs) - tl).reshape(nh, pages, page)
    total = run.reshape(nh, pages, page)[:, pages - 1:pages, :]
    tail[...] = tail[...] + jnp.broadcast_to(total, (nh, pages, page))

    def values_body(h, carry):
        w = zw[h]
        a = acc[h]
        for p in range(pages):
            a = a + w[p:p + 1, :] * v_refs[p][h]
        acc[h] = a
        return carry

    lax.fori_loop(0, nh, values_body, 0, unroll=2)

    @pl.when(g == pl.num_programs(1) - 1)
    def _():
        o_ref[...] = jnp.sum(acc[...], axis=2)


def sb_attention_decode(q, k_new, v_new, bias, cache_k, cache_v, page_table, pages=8):
    b, d = q.shape
    n_phys, page, nh, hd = cache_k.shape
    n_pages = page_table.shape[1]
    assert page == LANES and pages == 8 and n_pages % pages == 0
    ck = jnp.transpose(cache_k, (0, 2, 3, 1))
    cv = jnp.transpose(cache_v, (0, 2, 3, 1))
    blk = (None, nh, hd, page)
    seq = pl.BlockSpec(blk, lambda i, g, pt: (i, 0, 0, 0))

    def page_spec(p):
        return pl.BlockSpec(blk, lambda i, g, pt: (pt[i, n_pages - 1 - (g * pages + p)], 0, 0, 0))

    def cols(a):
        return jnp.broadcast_to(a.reshape(b, nh, hd, 1), (b, nh, hd, page))

    grid_spec = pltpu.PrefetchScalarGridSpec(
        num_scalar_prefetch=1,
        grid=(b, n_pages // pages),
        in_specs=[pl.BlockSpec(memory_space=pltpu.SMEM), seq, seq, seq]
        + [page_spec(p) for p in range(pages)] * 2,
        out_specs=pl.BlockSpec((None, nh, hd), lambda i, g, pt: (i, 0, 0)),
        scratch_shapes=[pltpu.VMEM((nh, hd, page), F32), pltpu.VMEM((nh, pages, page), F32),
                        pltpu.VMEM((nh, pages, page), F32)],
    )
    out = pl.pallas_call(
        functools.partial(_sb_decode_kernel, pages=pages, past_len=n_pages * page),
        grid_spec=grid_spec,
        out_shape=jax.ShapeDtypeStruct((b, nh, hd), F32),
        compiler_params=_params("arbitrary", "arbitrary"),
        name="sb_decode",
    )(page_table, bias.astype(F32), cols(q), cols(k_new), cols(v_new), *([ck] * pages), *([cv] * pages))
    return out.reshape(b, d)


def _proj_post_kernel(a_ref, w_ref, x_ref, g_ref, gate_ref, o_ref, *, paired):
    if paired:
        a = jnp.concatenate([a_ref[t] for t in range(a_ref.shape[0])], axis=1)
    else:
        a = a_ref[...]
    y = _dot(a.astype(BF16), w_ref[...])
    o_ref[...] = x_ref[...] + gate_ref[...] * _rms(y, g_ref[...])


def proj_post(a, w, x, g_all, g_idx, mod, gate_col, tm):
    m, d = x.shape
    paired = a.ndim == 4
    bpg = (m // tm) // mod.shape[0]
    if paired:
        spb = a.shape[2] // tm
        a_spec = pl.BlockSpec((None, a.shape[1], tm, PAIR), lambda i: (i // spb, 0, i % spb, 0))
    else:
        a_spec = pl.BlockSpec((tm, a.shape[1]), lambda i: (i, 0))
    return pl.pallas_call(
        functools.partial(_proj_post_kernel, paired=paired),
        grid=(m // tm,),
        in_specs=[a_spec, pl.BlockSpec(w.shape, lambda i: (0, 0)), pl.BlockSpec((tm, d), lambda i: (i, 0)),
                  _g_spec(d, g_idx), _mod_spec(mod, d, gate_col, bpg)],
        out_specs=pl.BlockSpec((tm, d), lambda i: (i, 0)),
        out_shape=jax.ShapeDtypeStruct((m, d), F32),
        compiler_params=_params("arbitrary"),
        name="proj_post",
    )(a, w, x, g_all, mod)


def _ffn_kernel(x_ref, g_ref, sh_ref, sc_ref, wg_ref, wu_ref, wd_ref, gp_ref, gate_ref, o_ref, h_scr, acc):
    f = pl.program_id(1)

    @pl.when(f == 0)
    def _():
        h_scr[...] = _normmod(x_ref[...], g_ref[...], sh_ref[...], sc_ref[...]).astype(BF16)
        acc[...] = jnp.zeros_like(acc)

    h = h_scr[...]
    act = _silu(_dot(h, wg_ref[...])) * _dot(h, wu_ref[...])
    acc[...] += _dot(act.astype(BF16), wd_ref[...])

    @pl.when(f == pl.num_programs(1) - 1)
    def _():
        o_ref[...] = x_ref[...] + gate_ref[...] * _rms(acc[...], gp_ref[...])


def ffn_sublayer(x, g_all, g_pre, g_post, mod, sh_col, sc_col, gate_col, w_gu, w_dn, tm, fc):
    m, d = x.shape
    dff = w_dn.shape[0]
    nf = dff // fc
    bpg = (m // tm) // mod.shape[0]
    return pl.pallas_call(
        _ffn_kernel,
        grid=(m // tm, nf),
        in_specs=[pl.BlockSpec((tm, d), lambda i, f: (i, 0)),
                  _g_spec(d, g_pre), _mod_spec(mod, d, sh_col, bpg), _mod_spec(mod, d, sc_col, bpg),
                  pl.BlockSpec((d, fc), lambda i, f: (0, f)),
                  pl.BlockSpec((d, fc), lambda i, f: (0, nf + f)),
                  pl.BlockSpec((fc, d), lambda i, f: (f, 0)),
                  _g_spec(d, g_post), _mod_spec(mod, d, gate_col, bpg)],
        out_specs=pl.BlockSpec((tm, d), lambda i, f: (i, 0)),
        out_shape=jax.ShapeDtypeStruct((m, d), F32),
        scratch_shapes=[pltpu.VMEM((tm, d), BF16), pltpu.VMEM((tm, d), F32)],
        compiler_params=_params("arbitrary", "arbitrary"),
        name="ffn",
    )(x, g_all, mod, mod, w_gu, w_gu, w_dn, g_all, mod)


def _expand_heads(a, e_bf16):
    return _split_dot(a, e_bf16)


def _ssd_prompt_kernel(z_ref, x_ref, bc_ref, dt_ref, cwx_ref, cwbc_ref, cbx_ref, cbbc_ref, dtb_ref, a_ref,
                       dsk_ref, ng_ref, e_ref, y_ref, st_ref, conv_ref, xs_x, xs_bc, y_scr,
                       *, chunk, groups, dstate):
    c = pl.program_id(1)
    last = c == pl.num_programs(1) - 1
    L = chunk
    d_inner = x_ref.shape[1]
    npairs = d_inner // PAIR
    ppg = npairs // groups
    kw = cwx_ref.shape[0]

    @pl.when(c == 0)
    def _():
        xs_x[0:8, :] = jnp.zeros((8, xs_x.shape[1]), F32)
        xs_bc[0:8, :] = jnp.zeros((8, xs_bc.shape[1]), F32)
        st_ref[...] = jnp.zeros_like(st_ref)

    def conv_act(cur_ref, xs, cw_ref, cb_ref):
        xs[8:8 + L, :] = cur_ref[...]
        out = cb_ref[...] + cw_ref[kw - 1:kw, :] * cur_ref[...]
        for k in range(kw - 1):
            sft = kw - 1 - k
            out = out + cw_ref[k:k + 1, :] * xs[8 - sft:8 - sft + L, :]
        return _silu(out)

    x_act = conv_act(x_ref, xs_x, cwx_ref, cbx_ref)
    bc_act = conv_act(bc_ref, xs_bc, cwbc_ref, cbbc_ref)

    @pl.when(last)
    def _():
        conv_ref[:, 0:d_inner] = xs_x[8 + L - (kw - 1):8 + L, :]
        conv_ref[:, d_inner:] = xs_bc[8 + L - (kw - 1):8 + L, :]

    xs_x[0:8, :] = xs_x[L:L + 8, :]
    xs_bc[0:8, :] = xs_bc[L:L + 8, :]

    dt = _softplus(dt_ref[...] + dtb_ref[...])
    dta = dt * a_ref[...]
    r_ = lax.broadcasted_iota(jnp.int32, (L, L), 0)
    c_ = lax.broadcasted_iota(jnp.int32, (L, L), 1)
    causal = r_ >= c_
    a_cs = _dot_hi(causal.astype(F32), dta)
    a_cs_t = a_cs.T
    a_end = a_cs[L - 1:L, :]
    e = e_ref[...]
    dt_x = _expand_heads(dt, e)
    grow_x = _expand_heads(jnp.exp(a_cs), e)
    dend_x = _expand_heads(jnp.exp(a_end - a_cs), e)
    cdec = jnp.exp(a_end)

    low = lax.broadcasted_iota(jnp.int32, (1, PAIR), 1) < HEAD_DIM
    rlow = lax.broadcasted_iota(jnp.int32, (PAIR, 1), 0) < HEAD_DIM
    gw = groups * dstate
    for g in range(groups):
        bg = bc_act[:, g * dstate:(g + 1) * dstate].astype(BF16)
        cg = bc_act[:, gw + g * dstate:gw + (g + 1) * dstate].astype(BF16)
        cb = lax.dot_general(cg, bg, NT, preferred_element_type=F32)
        for pp in range(ppg):
            hp = g * ppg + pp
            sl = slice(hp * PAIR, (hp + 1) * PAIR)
            xdt = x_act[:, sl] * dt_x[:, sl]
            xdt_b = xdt.astype(BF16)
            y = jnp.zeros((L, PAIR), F32)
            for hh in range(2):
                h = 2 * hp + hh
                seg = a_cs[:, h:h + 1] - a_cs_t[h:h + 1, :]
                dec = jnp.where(causal, jnp.exp(jnp.where(causal, seg, 0.0)), 0.0)
                sel = low if hh == 0 else jnp.logical_not(low)
                y = y + _dot((cb * dec).astype(BF16), jnp.where(sel, xdt_b, jnp.zeros_like(xdt_b)))
            prev = st_ref[hp]
            y = y + lax.dot_general(cg, prev.astype(BF16), NT, preferred_element_type=F32) * grow_x[:, sl]
            y_scr[:, sl] = y
            xw = (xdt * dend_x[:, sl]).astype(BF16)
            s_c = lax.dot_general(xw, bg, TN, preferred_element_type=F32)
            cd = jnp.where(rlow, cdec[:, 2 * hp:2 * hp + 1], cdec[:, 2 * hp + 1:2 * hp + 2])
            st_ref[hp] = prev * cd + s_c

    y = y_scr[...] + x_act * dsk_ref[...]
    y = y * _silu(z_ref[...])
    gsz = d_inner // groups
    for g in range(groups):
        sl = slice(g * gsz, (g + 1) * gsz)
        y_ref[:, sl] = _rms(y[:, sl], ng_ref[:, sl]).astype(BF16)


def _head_expander(nheads):
    lane = jnp.arange(nheads * HEAD_DIM)[None, :] // HEAD_DIM
    return (lane == jnp.arange(LANES)[:, None]).astype(BF16)


def ssd_prompt(zx, dt_raw, conv_w, conv_b, dt_bias, a_neg, d_skip, norm_g, batch, chunk, groups, dstate):
    m = zx.shape[0]
    d_inner = d_skip.shape[0] * HEAD_DIM
    nheads = d_skip.shape[0]
    bcw = 2 * groups * dstate
    s = m // batch
    nc = s // chunk
    kw = conv_w.shape[0]
    npairs = d_inner // PAIR
    pad = LANES - nheads
    dtb = jnp.pad(dt_bias.astype(F32), (0, pad)).reshape(1, LANES)
    a_p = jnp.pad(a_neg, (0, pad)).reshape(1, LANES)
    dsk = jnp.repeat(d_skip.astype(F32), HEAD_DIM).reshape(1, d_inner)
    cb2 = conv_b.reshape(1, -1)
    rb = lambda b, c: b * nc + c
    const = lambda shape, j=0: pl.BlockSpec(shape, lambda b, c: (0, j))
    return pl.pallas_call(
        functools.partial(_ssd_prompt_kernel, chunk=chunk, groups=groups, dstate=dstate),
        grid=(batch, nc),
        in_specs=[pl.BlockSpec((chunk, d_inner), lambda b, c: (rb(b, c), 0)),
                  pl.BlockSpec((chunk, d_inner), lambda b, c: (rb(b, c), 1)),
                  pl.BlockSpec((chunk, bcw), lambda b, c: (rb(b, c), 2 * d_inner // bcw)),
                  pl.BlockSpec((chunk, LANES), lambda b, c: (rb(b, c), 0)),
                  const((kw, d_inner)), const((kw, bcw), d_inner // bcw),
                  const((1, d_inner)), const((1, bcw), d_inner // bcw),
                  const((1, LANES)), const((1, LANES)), const((1, d_inner)), const((1, d_inner)),
                  const((LANES, d_inner))],
        out_specs=[pl.BlockSpec((chunk, d_inner), lambda b, c: (rb(b, c), 0)),
                   pl.BlockSpec((None, npairs, PAIR, dstate), lambda b, c: (b, 0, 0, 0)),
                   pl.BlockSpec((None, kw - 1, d_inner + bcw), lambda b, c: (b, 0, 0))],
        out_shape=[jax.ShapeDtypeStruct((m, d_inner), BF16),
                   jax.ShapeDtypeStruct((batch, npairs, PAIR, dstate), F32),
                   jax.ShapeDtypeStruct((batch, kw - 1, d_inner + bcw), F32)],
        scratch_shapes=[pltpu.VMEM((chunk + 8, d_inner), F32), pltpu.VMEM((chunk + 8, bcw), F32),
                        pltpu.VMEM((chunk, d_inner), F32)],
        compiler_params=_params("arbitrary", "arbitrary"),
        name="ssd_prompt",
    )(zx, zx, zx, dt_raw, conv_w, conv_w, cb2, cb2, dtb, a_p, dsk, norm_g.reshape(1, d_inner),
      _head_expander(nheads))


def _ssd_step_kernel(z_ref, x_ref, bc_ref, dt_ref, sx_ref, sbc_ref, st_ref, cwx_ref, cwbc_ref, cbx_ref, cbbc_ref,
                     dtb_ref, a_ref, dsk_ref, ng_ref, y_ref, sto_ref, conv_ref, y_scr, *, groups, dstate):
    d_inner = x_ref.shape[1]
    npairs = d_inner // PAIR
    ppg = npairs // groups
    kw = cwx_ref.shape[0]

    def conv_act(new_ref, s_ref, cw_ref, cb_ref):
        out = cb_ref[...] + cw_ref[kw - 1:kw, :] * new_ref[...]
        for k in range(kw - 1):
            out = out + cw_ref[k:k + 1, :] * s_ref[k:k + 1, :]
        return _silu(out)

    x_act = conv_act(x_ref, sx_ref, cwx_ref, cbx_ref)
    bc_act = conv_act(bc_ref, sbc_ref, cwbc_ref, cbbc_ref)
    conv_ref[0:kw - 2, 0:d_inner] = sx_ref[1:kw - 1, :]
    conv_ref[0:kw - 2, d_inner:] = sbc_ref[1:kw - 1, :]
    conv_ref[kw - 2:kw - 1, 0:d_inner] = x_ref[...]
    conv_ref[kw - 2:kw - 1, d_inner:] = bc_ref[...]

    dt = _softplus(dt_ref[...] + dtb_ref[...])
    decay = jnp.exp(dt * a_ref[...])
    rlow = lax.broadcasted_iota(jnp.int32, (PAIR, 1), 0) < HEAD_DIM
    low = lax.broadcasted_iota(jnp.int32, (1, PAIR), 1) < HEAD_DIM
    eye = lax.broadcasted_iota(jnp.int32, (PAIR, PAIR), 0) == lax.broadcasted_iota(jnp.int32, (PAIR, PAIR), 1)
    gw = groups * dstate
    for hp in range(npairs):
        g = hp // ppg
        sl = slice(hp * PAIR, (hp + 1) * PAIR)
        bg = jnp.broadcast_to(bc_act[:, g * dstate:(g + 1) * dstate], (PAIR, dstate))
        cg = jnp.broadcast_to(bc_act[:, gw + g * dstate:gw + (g + 1) * dstate], (8, dstate))
        dt_p = jnp.where(low, dt[:, 2 * hp:2 * hp + 1], dt[:, 2 * hp + 1:2 * hp + 2])
        xdt = x_act[:, sl] * dt_p
        outer = _dot_hi(jnp.where(eye, jnp.broadcast_to(xdt, (PAIR, PAIR)), 0.0), bg)
        dc = jnp.where(rlow, decay[:, 2 * hp:2 * hp + 1], decay[:, 2 * hp + 1:2 * hp + 2])
        new = st_ref[hp] * dc + outer
        sto_ref[hp] = new
        y_scr[:, sl] = _dot_hi(cg, new, NT)[0:1, :]

    y = y_scr[...] + x_act * dsk_ref[...]
    y = y * _silu(z_ref[...])
    gsz = d_inner // groups
    for g in range(groups):
        sl = slice(g * gsz, (g + 1) * gsz)
        y_ref[:, sl] = _rms(y[:, sl], ng_ref[:, sl]).astype(BF16)


def ssd_step(zx, dt_raw, state_conv, state_ssm, conv_w, conv_b, dt_bias, a_neg, d_skip, norm_g, groups, dstate):
    b = zx.shape[0]
    nheads = d_skip.shape[0]
    d_inner = nheads * HEAD_DIM
    bcw = 2 * groups * dstate
    kw = conv_w.shape[0]
    npairs = d_inner // PAIR
    pad = LANES - nheads
    dtb = jnp.pad(dt_bias.astype(F32), (0, pad)).reshape(1, LANES)
    a_p = jnp.pad(a_neg, (0, pad)).reshape(1, LANES)
    dsk = jnp.repeat(d_skip.astype(F32), HEAD_DIM).reshape(1, d_inner)
    cb2 = conv_b.reshape(1, -1)
    zx3 = zx.reshape(b, 1, -1)
    st = state_ssm.reshape(b, npairs, PAIR, dstate)
    const = lambda shape, j=0: pl.BlockSpec(shape, lambda i: (0, j))
    y, st_new, conv_new = pl.pallas_call(
        functools.partial(_ssd_step_kernel, groups=groups, dstate=dstate),
        grid=(b,),
        in_specs=[pl.BlockSpec((None, 1, d_inner), lambda i: (i, 0, 0)),
                  pl.BlockSpec((None, 1, d_inner), lambda i: (i, 0, 1)),
                  pl.BlockSpec((None, 1, bcw), lambda i: (i, 0, 2 * d_inner // bcw)),
                  pl.BlockSpec((None, 1, LANES), lambda i: (i, 0, 0)),
                  pl.BlockSpec((None, kw - 1, d_inner), lambda i: (i, 0, 0)),
                  pl.BlockSpec((None, kw - 1, bcw), lambda i: (i, 0, d_inner // bcw)),
                  pl.BlockSpec((None, npairs, PAIR, dstate), lambda i: (i, 0, 0, 0)),
                  const((kw, d_inner)), const((kw, bcw), d_inner // bcw),
                  const((1, d_inner)), const((1, bcw), d_inner // bcw),
                  const((1, LANES)), const((1, LANES)), const((1, d_inner)), const((1, d_inner))],
        out_specs=[pl.BlockSpec((None, 1, d_inner), lambda i: (i, 0, 0)),
                   pl.BlockSpec((None, npairs, PAIR, dstate), lambda i: (i, 0, 0, 0)),
                   pl.BlockSpec((None, kw - 1, d_inner + bcw), lambda i: (i, 0, 0))],
        out_shape=[jax.ShapeDtypeStruct((b, 1, d_inner), BF16),
                   jax.ShapeDtypeStruct((b, npairs, PAIR, dstate), F32),
                   jax.ShapeDtypeStruct((b, kw - 1, d_inner + bcw), F32)],
        scratch_shapes=[pltpu.VMEM((1, d_inner), F32)],
        compiler_params=_params("arbitrary"),
        name="ssd_step",
    )(zx3, zx3, zx3, dt_raw.reshape(b, 1, LANES), state_conv, state_conv, st, conv_w, conv_w, cb2, cb2,
      dtb, a_p, dsk, norm_g.reshape(1, d_inner))
    return y.reshape(b, d_inner), st_new, conv_new


def _router_kernel(x_ref, g_ref, sh_ref, sc_ref, wr_ref, h_ref, route_ref, *, n_experts):
    h = _normmod(x_ref[...], g_ref[...], sh_ref[...], sc_ref[...])
    h_ref[...] = h
    logits = _dot_hi(h, wr_ref[...])
    lane = lax.broadcasted_iota(jnp.int32, logits.shape, 1).astype(F32)
    neg = jnp.float32(-jnp.inf)
    lg = jnp.where(lane < n_experts, logits, neg)
    m1 = jnp.max(lg, axis=1, keepdims=True)
    i1 = jnp.min(jnp.where(lg == m1, lane, float(LANES)), axis=1, keepdims=True)
    lg2 = jnp.where(lane == i1, neg, lg)
    m2 = jnp.max(lg2, axis=1, keepdims=True)
    i2 = jnp.min(jnp.where(lg2 == m2, lane, float(LANES)), axis=1, keepdims=True)
    e2 = jnp.exp(m2 - m1)
    g1 = 1.0 / (1.0 + e2)
    route_ref[...] = (jnp.where(lane == 0.0, i1, 0.0) + jnp.where(lane == 1.0, i2, 0.0)
                      + jnp.where(lane == 2.0, g1, 0.0) + jnp.where(lane == 3.0, e2 * g1, 0.0))


def moe_router(x, g_all, g_idx, mod, sh_col, sc_col, w_router, tm):
    m, d = x.shape
    n_experts = w_router.shape[1]
    bpg = (m // tm) // mod.shape[0]
    wr = jnp.pad(w_router.astype(F32), ((0, 0), (0, LANES - n_experts)))
    return pl.pallas_call(
        functools.partial(_router_kernel, n_experts=n_experts),
        grid=(m // tm,),
        in_specs=[pl.BlockSpec((tm, d), lambda i: (i, 0)),
                  _g_spec(d, g_idx), _mod_spec(mod, d, sh_col, bpg), _mod_spec(mod, d, sc_col, bpg),
                  pl.BlockSpec((d, LANES), lambda i: (0, 0))],
        out_specs=[pl.BlockSpec((tm, d), lambda i: (i, 0)), pl.BlockSpec((tm, LANES), lambda i: (i, 0))],
        out_shape=[jax.ShapeDtypeStruct((m, d), F32), jax.ShapeDtypeStruct((m, LANES), F32)],
        compiler_params=_params("arbitrary"),
        name="moe_router",
    )(x, g_all, mod, mod, wr)


def moe_plan(routes, n_experts, tm):
    t = routes.shape[0]
    e = routes[:, :2].astype(jnp.int32).reshape(-1)
    onehot = (e[:, None] == jnp.arange(n_experts, dtype=jnp.int32)[None, :]).astype(jnp.int32)
    before = jnp.cumsum(onehot, axis=0) - onehot
    counts = jnp.sum(onehot, axis=0)
    padded = (counts + tm - 1) // tm * tm
    ends = jnp.cumsum(padded)
    pos = jnp.sum(onehot * (before + (ends - padded)[None, :]), axis=1)
    nb = -(-(2 * t + n_experts * (tm - 1)) // tm)
    blk_start = jnp.arange(nb, dtype=jnp.int32) * tm
    blk_e = jnp.minimum(jnp.sum((blk_start[:, None] >= ends[None, :]).astype(jnp.int32), axis=1), n_experts - 1)
    live = (blk_start < ends[-1]).astype(jnp.int32)
    blk_e = jnp.where(live == 1, blk_e, blk_e[jnp.maximum(jnp.sum(live) - 1, 0)])
    return pos.astype(jnp.int32), blk_e.astype(jnp.int32), live


def _slot_spec(tb, copies):
    return pl.BlockSpec((None, 1, copies * tb), lambda i, *_: (i, 0, 0), memory_space=pltpu.SMEM)


def _dispatch_kernel(pos_ref, h_ref, buf_in_ref, buf_ref, sem, *, copies):
    del buf_in_ref
    n = copies * h_ref.shape[0]

    def row_copy(i):
        return pltpu.make_async_copy(h_ref.at[pl.ds(i // copies, 1)], buf_ref.at[pl.ds(pos_ref[0, i], 1)], sem)

    def start(i, c):
        row_copy(i).start()
        return c

    def wait(i, c):
        row_copy(i).wait()
        return c

    lax.fori_loop(0, n, start, 0, unroll=8)
    lax.fori_loop(0, n, wait, 0, unroll=8)


def moe_dispatch(h, pos, buf, tb, copies=2):
    m, d = h.shape
    return pl.pallas_call(
        functools.partial(_dispatch_kernel, copies=copies),
        grid=(m // tb,),
        in_specs=[_slot_spec(tb, copies), pl.BlockSpec((tb, d), lambda i: (i, 0)),
                  pl.BlockSpec(memory_space=pl.ANY)],
        out_specs=pl.BlockSpec(memory_space=pl.ANY),
        out_shape=jax.ShapeDtypeStruct(buf.shape, buf.dtype),
        scratch_shapes=[pltpu.SemaphoreType.DMA(())],
        input_output_aliases={2: 0},
        compiler_params=_params("arbitrary"),
        name="moe_dispatch",
    )(pos.reshape(m // tb, 1, copies * tb), h, buf)


def _moe_experts_kernel(be_ref, live_ref, h_ref, wg_ref, wu_ref, wd_ref, y_ref, hb, acc):
    f = pl.program_id(1)
    live = live_ref[pl.program_id(0)] == 1

    @pl.when(live & (f == 0))
    def _():
        hb[...] = h_ref[...].astype(BF16)
        acc[...] = jnp.zeros_like(acc)

    @pl.when(live)
    def _():
        h = hb[...]
        act = _silu(_dot(h, wg_ref[...])) * _dot(h, wu_ref[...])
        acc[...] += _dot(act.astype(BF16), wd_ref[...])

    @pl.when(f == pl.num_programs(1) - 1)
    def _():
        y_ref[...] = jnp.where(live, acc[...], 0.0)


def moe_experts(h_sorted, blk_e, live, w_gu, w_dn, tm, fc):
    p, d = h_sorted.shape
    dff = w_dn.shape[1]
    nf = dff // fc

    def fe(b, f, live_):
        return jnp.where(live_[b] == 1, f, nf - 1)

    grid_spec = pltpu.PrefetchScalarGridSpec(
        num_scalar_prefetch=2,
        grid=(p // tm, nf),
        in_specs=[pl.BlockSpec((tm, d), lambda b, f, be, lv: (b, 0)),
                  pl.BlockSpec((None, d, fc), lambda b, f, be, lv: (be[b], 0, fe(b, f, lv))),
                  pl.BlockSpec((None, d, fc), lambda b, f, be, lv: (be[b], 0, nf + fe(b, f, lv))),
                  pl.BlockSpec((None, fc, d), lambda b, f, be, lv: (be[b], fe(b, f, lv), 0))],
        out_specs=pl.BlockSpec((tm, d), lambda b, f, be, lv: (b, 0)),
        scratch_shapes=[pltpu.VMEM((tm, d), BF16), pltpu.VMEM((tm, d), F32)],
    )
    return pl.pallas_call(
        _moe_experts_kernel,
        grid_spec=grid_spec,
        out_shape=jax.ShapeDtypeStruct((p, d), F32),
        compiler_params=_params("arbitrary", "arbitrary"),
        name="moe_experts",
    )(blk_e, live, h_sorted, w_gu, w_gu, w_dn)


def _moe_combine_kernel(pos_ref, y_hbm, route_ref, x_ref, gp_ref, gate_ref, o_ref, ybuf, sem):
    tb = x_ref.shape[0]

    def row_copy(i):
        return pltpu.make_async_copy(y_hbm.at[pl.ds(pos_ref[0, i], 1)], ybuf.at[i % 2, pl.ds(i // 2, 1)], sem)

    def start(i, c):
        row_copy(i).start()
        return c

    def wait(i, c):
        row_copy(i).wait()
        return c

    lax.fori_loop(0, 2 * tb, start, 0, unroll=8)
    lax.fori_loop(0, 2 * tb, wait, 0, unroll=8)
    r = route_ref[...]
    lane = lax.broadcasted_iota(jnp.int32, r.shape, 1)
    g1 = jnp.sum(jnp.where(lane == 2, r, 0.0), axis=1, keepdims=True)
    g2 = jnp.sum(jnp.where(lane == 3, r, 0.0), axis=1, keepdims=True)
    y = g1 * ybuf[0] + g2 * ybuf[1]
    o_ref[...] = x_ref[...] + gate_ref[...] * _rms(y, gp_ref[...])


def moe_combine(y_sorted, pos, routes, x, g_all, g_post, mod, gate_col, tb):
    m, d = x.shape
    bpg = (m // tb) // mod.shape[0]
    return pl.pallas_call(
        _moe_combine_kernel,
        grid=(m // tb,),
        in_specs=[_slot_spec(tb, 2), pl.BlockSpec(memory_space=pl.ANY),
                  pl.BlockSpec((tb, LANES), lambda i: (i, 0)), pl.BlockSpec((tb, d), lambda i: (i, 0)),
                  _g_spec(d, g_post), _mod_spec(mod, d, gate_col, bpg)],
        out_specs=pl.BlockSpec((tb, d), lambda i: (i, 0)),
        out_shape=jax.ShapeDtypeStruct((m, d), F32),
        scratch_shapes=[pltpu.VMEM((2, tb, d), F32), pltpu.SemaphoreType.DMA(())],
        compiler_params=_params("arbitrary"),
        name="moe_combine",
    )(pos.reshape(m // tb, 1, 2 * tb), y_sorted, routes, x, g_all, mod)


def kernel(x_prompt, x_sample, cache_k, cache_v, state_ssm, state_conv, page_table, c_prompt, c_sample, norm_g, w_ada, b_ada, sb_w_qkv, sb_w_o, sb_logit_bias, ssd_w_in, ssd_conv_w, ssd_conv_b, ssd_dt_bias, ssd_a_log, ssd_d, ssd_norm_g, ssd_w_out, ffn_w_gu, ffn_w_dn, moe_w_router, moe_w_gu, moe_w_dn):
    bp, seq, d = x_prompt.shape
    bs = x_sample.shape[0]
    nh_sb = sb_logit_bias.shape[0]
    nh_ssd = ssd_d.shape[0]
    d_inner = nh_ssd * HEAD_DIM
    dstate = state_ssm.shape[-1]
    groups = (ssd_conv_w.shape[1] - d_inner) // (2 * dstate)
    n_zx = d_inner + ssd_conv_w.shape[1]
    mp = bp * seq
    tm = min(512, seq)
    tm_big = min(1024, seq)

    c_all = jnp.concatenate([c_prompt, c_sample], axis=0)
    c_all = jnp.pad(c_all, ((0, -c_all.shape[0] % 8), (0, 0)))
    ada = adaln_all(c_all, w_ada, b_ada)
    g_all = norm_g.reshape(-1, 1, d)

    log2e = 1.4426950408889634
    qscale = jnp.concatenate([jnp.full((d,), log2e * HEAD_DIM ** -0.5, F32), jnp.ones((2 * d,), F32)])
    sb_bias = sb_logit_bias.astype(F32) * log2e
    w_qkv = (sb_w_qkv * qscale).astype(BF16)
    w_o = sb_w_o.astype(BF16)
    w_gu = ffn_w_gu.astype(BF16)
    w_dn = ffn_w_dn.astype(BF16)
    w_in = ssd_w_in[:, :n_zx].astype(BF16)
    w_dt = jnp.pad(ssd_w_in[:, n_zx:], ((0, 0), (0, LANES - nh_ssd))).astype(BF16)
    w_out = ssd_w_out.astype(BF16)
    mw_gu = moe_w_gu.astype(BF16)
    mw_dn = moe_w_dn.astype(BF16)
    a_neg = -jnp.exp(ssd_a_log.astype(F32))

    xp = x_prompt.reshape(mp, d)
    xs = x_sample.reshape(bs, d)

    mod_p = ada[0, :bp].reshape(bp, 1, 6 * d)
    mod_s = ada[0, bp:bp + bs].reshape(1, bs, 6 * d)
    q_pairs, k_t, v_t, kv_t = qkv_prompt(xp, g_all, 0, mod_p, 0, 1, w_qkv, w_qkv[:, d:].T, bp, tm_big, tn=512)
    o_p = sb_attention_prompt(q_pairs, kv_t, sb_bias, t=min(256, seq))
    xp = proj_post(o_p, w_o, xp, g_all, 1, mod_p, 2, tm)
    xp = ffn_sublayer(xp, g_all, 2, 3, mod_p, 3, 4, 5, w_gu, w_dn, tm, w_dn.shape[0] // 2)

    qkv_s = nm_matmul(xs, g_all, 0, mod_s, 0, 1, w_qkv, bs, 512)
    q_s, k_s, v_s = qkv_s[:, :d], qkv_s[:, d:2 * d], qkv_s[:, 2 * d:]
    o_s = sb_attention_decode(q_s, k_s, v_s, sb_bias, cache_k, cache_v, page_table)
    xs = proj_post(o_s, w_o, xs, g_all, 1, mod_s, 2, bs)
    xs = ffn_sublayer(xs, g_all, 2, 3, mod_s, 3, 4, 5, w_gu, w_dn, bs, w_dn.shape[0] // 2)

    mod_p = ada[1, :bp].reshape(bp, 1, 6 * d)
    mod_s = ada[1, bp:bp + bs].reshape(1, bs, 6 * d)
    zx_p, dt_p = nm_matmul(xp, g_all, 4, mod_p, 0, 1, w_in, tm_big, n_zx // 4, extra_w=(w_dt,))
    y_p, ssm_p, conv_p = ssd_prompt(zx_p, dt_p, ssd_conv_w, ssd_conv_b, ssd_dt_bias, a_neg, ssd_d, ssd_norm_g,
                                    bp, min(128, seq), groups, dstate)
    xp = proj_post(y_p, w_out, xp, g_all, 5, mod_p, 2, tm)

    zx_s, dt_s = nm_matmul(xs, g_all, 4, mod_s, 0, 1, w_in, bs, 512, extra_w=(w_dt,))
    y_s, ssm_s, conv_s = ssd_step(zx_s, dt_s, state_conv, state_ssm, ssd_conv_w, ssd_conv_b, ssd_dt_bias, a_neg,
                                  ssd_d, ssd_norm_g, groups, dstate)
    xs = proj_post(y_s, w_out, xs, g_all, 5, mod_s, 2, bs)

    n_experts = moe_w_router.shape[1]
    tb = min(256, seq)
    h_p, route_p = moe_router(xp, g_all, 6, mod_p, 3, 4, moe_w_router, tm)
    h_s, route_s = moe_router(xs, g_all, 6, mod_s, 3, 4, moe_w_router, bs)
    pos, blk_e, live = moe_plan(jnp.concatenate([route_p, route_s], axis=0), n_experts, tm)
    h_sorted = jnp.zeros((blk_e.shape[0] * tm, d), F32)
    h_sorted = moe_dispatch(h_p, pos[:2 * mp], h_sorted, tb)
    h_sorted = moe_dispatch(h_s, pos[2 * mp:], h_sorted, bs)
    y_sorted = moe_experts(h_sorted, blk_e, live, mw_gu, mw_dn, tm, mw_dn.shape[1] // 4)
    xp = moe_combine(y_sorted, pos[:2 * mp], route_p, xp, g_all, 7, mod_p, 5, tb)
    xs = moe_combine(y_sorted, pos[2 * mp:], route_s, xs, g_all, 7, mod_s, 5, bs)

    hd = (nh_sb, d // nh_sb)
    return (xp.reshape(bp, seq, d), xs.reshape(bs, 1, d),
            jnp.transpose(k_t, (0, 3, 1, 2)), jnp.transpose(v_t, (0, 3, 1, 2)),
            k_s.reshape(bs, 1, *hd), v_s.reshape(bs, 1, *hd),
            ssm_p.reshape(bp, nh_ssd, HEAD_DIM, dstate), conv_p,
            ssm_s.reshape(bs, nh_ssd, HEAD_DIM, dstate), conv_s)
```

```python
import functools

import jax
import jax.numpy as jnp
from jax import lax
from jax.experimental import pallas as pl
from jax.experimental.pallas import tpu as pltpu

F32 = jnp.float32
BF16 = jnp.bfloat16
EPS = 1e-6
LANES = 128
HEAD_DIM = 64
PAIR = 2 * HEAD_DIM
VMEM_LIMIT = 56 * 1024 * 1024

NT = (((1,), (1,)), ((), ()))
TN = (((0,), (0,)), ((), ()))


def _params(*sem):
    return pltpu.CompilerParams(dimension_semantics=sem, vmem_limit_bytes=VMEM_LIMIT)


def _silu(x):
    return (0.5 * x) * (1.0 + jnp.tanh(0.5 * x))


def _softplus(z):
    return jnp.maximum(z, 0.0) + jnp.log(1.0 + jnp.exp(-jnp.abs(z)))


def _sb_logits(z):
    sp = jnp.maximum(z, 0.0) + jnp.log2(1.0 + jnp.exp2(jnp.minimum(z, -z)))
    return sp, z - sp


def _rms(x, g):
    return x * lax.rsqrt(jnp.mean(x * x, axis=-1, keepdims=True) + EPS) * g


def _normmod(x, g, shift, scale):
    return _rms(x, g) * (1.0 + scale) + shift


def _dot(a, b):
    return jnp.dot(a, b, preferred_element_type=F32)


def _dot_hi(a, b, dims=None):
    if dims is None:
        return jnp.dot(a, b, preferred_element_type=F32, precision=lax.Precision.HIGHEST)
    return lax.dot_general(a, b, dims, preferred_element_type=F32, precision=lax.Precision.HIGHEST)


def _split_dot(a, b_bf16):
    hi = a.astype(BF16)
    lo = (a - hi.astype(F32)).astype(BF16)
    return _dot(hi, b_bf16) + _dot(lo, b_bf16)


def _mod_spec(mod, d, col, bpg):
    r = mod.shape[1]
    return pl.BlockSpec((None, r, d), lambda i, *_: (i // bpg, 0, col))


def _g_spec(d, idx):
    return pl.BlockSpec((None, 1, d), lambda i, *_: (idx, 0, 0))


def _adaln_kernel(c_ref, w_ref, b_ref, o_ref):
    s = _silu(c_ref[...]).astype(BF16)
    o_ref[...] = _dot(s, w_ref[...].astype(BF16)) + b_ref[...]


def adaln_all(c, w_ada, b_ada, tn=1024):
    depth, d, n = w_ada.shape
    m = c.shape[0]
    return pl.pallas_call(
        _adaln_kernel,
        grid=(depth, n // tn),
        in_specs=[pl.BlockSpec((m, d), lambda l, j: (0, 0)),
                  pl.BlockSpec((None, d, tn), lambda l, j: (l, 0, j)),
                  pl.BlockSpec((None, 1, tn), lambda l, j: (l, 0, j))],
        out_specs=pl.BlockSpec((None, m, tn), lambda l, j: (l, 0, j)),
        out_shape=jax.ShapeDtypeStruct((depth, m, n), F32),
        compiler_params=_params("arbitrary", "arbitrary"),
        name="adaln",
    )(c, w_ada, b_ada.reshape(depth, 1, n))


def _nm_matmul_kernel(x_ref, g_ref, sh_ref, sc_ref, w_ref, *rest, n_extra):
    extra_w = rest[:n_extra]
    o_ref = rest[n_extra]
    extra_o = rest[n_extra + 1:2 * n_extra + 1]
    h_scr = rest[-1]

    @pl.when(pl.program_id(1) == 0)
    def _():
        h_scr[...] = _normmod(x_ref[...], g_ref[...], sh_ref[...], sc_ref[...]).astype(BF16)
        for w2, o2 in zip(extra_w, extra_o):
            o2[...] = _dot(h_scr[...], w2[...])

    o_ref[...] = _dot(h_scr[...], w_ref[...])


def nm_matmul(x, g_all, g_idx, mod, sh_col, sc_col, w, tm, tn, extra_w=()):
    m, d = x.shape
    n = w.shape[1]
    bpg = (m // tm) // mod.shape[0]
    in_specs = [pl.BlockSpec((tm, d), lambda i, j: (i, 0)),
                _g_spec(d, g_idx), _mod_spec(mod, d, sh_col, bpg), _mod_spec(mod, d, sc_col, bpg),
                pl.BlockSpec((d, tn), lambda i, j: (0, j))]
    out_specs = [pl.BlockSpec((tm, tn), lambda i, j: (i, j))]
    out_shape = [jax.ShapeDtypeStruct((m, n), F32)]
    for w2 in extra_w:
        in_specs.append(pl.BlockSpec(w2.shape, lambda i, j: (0, 0)))
        out_specs.append(pl.BlockSpec((tm, w2.shape[1]), lambda i, j: (i, 0)))
        out_shape.append(jax.ShapeDtypeStruct((m, w2.shape[1]), F32))
    outs = pl.pallas_call(
        functools.partial(_nm_matmul_kernel, n_extra=len(extra_w)),
        grid=(m // tm, n // tn),
        in_specs=in_specs, out_specs=out_specs, out_shape=out_shape,
        scratch_shapes=[pltpu.VMEM((tm, d), BF16)],
        compiler_params=_params("arbitrary", "arbitrary"),
        name="nm_matmul",
    )(x, g_all, mod, mod, w, *extra_w)
    return outs if extra_w else outs[0]


def _qkv_kernel(x_ref, g_ref, sh_ref, sc_ref, wq_ref, wkv_ref, qp_ref, kt_ref, vt_ref, kvb_ref):
    d = x_ref.shape[1]
    npairs = d // PAIR
    h = _normmod(x_ref[...], g_ref[...], sh_ref[...], sc_ref[...]).astype(BF16)
    r = _dot(h, wq_ref[...])
    for t in range(npairs):
        qp_ref[t] = r[:, t * PAIR:(t + 1) * PAIR].astype(BF16)
    rt = lax.dot_general(wkv_ref[...], h, NT, preferred_element_type=F32)
    for t in range(d // HEAD_DIM):
        kt_ref[t] = rt[t * HEAD_DIM:(t + 1) * HEAD_DIM, :]
        vt_ref[t] = rt[d + t * HEAD_DIM:d + (t + 1) * HEAD_DIM, :]
    for t in range(2 * npairs):
        kvb_ref[t // npairs, t % npairs] = rt[t * PAIR:(t + 1) * PAIR, :].astype(BF16)


def qkv_prompt(x, g_all, g_idx, mod, sh_col, sc_col, w_q, w_kv_t, batch, tm):
    m, d = x.shape
    s = m // batch
    bpg = s // tm
    nh = d // HEAD_DIM
    npairs = d // PAIR
    return pl.pallas_call(
        _qkv_kernel,
        grid=(m // tm,),
        in_specs=[pl.BlockSpec((tm, d), lambda i: (i, 0)),
                  _g_spec(d, g_idx), _mod_spec(mod, d, sh_col, bpg), _mod_spec(mod, d, sc_col, bpg),
                  pl.BlockSpec((d, d), lambda i: (0, 0)),
                  pl.BlockSpec((2 * d, d), lambda i: (0, 0))],
        out_specs=[pl.BlockSpec((None, npairs, tm, PAIR), lambda i: (i // bpg, 0, i % bpg, 0)),
                   pl.BlockSpec((None, nh, HEAD_DIM, tm), lambda i: (i // bpg, 0, 0, i % bpg)),
                   pl.BlockSpec((None, nh, HEAD_DIM, tm), lambda i: (i // bpg, 0, 0, i % bpg)),
                   pl.BlockSpec((None, 2, npairs, PAIR, tm), lambda i: (i // bpg, 0, 0, 0, i % bpg))],
        out_shape=[jax.ShapeDtypeStruct((batch, npairs, s, PAIR), BF16),
                   jax.ShapeDtypeStruct((batch, nh, HEAD_DIM, s), F32),
                   jax.ShapeDtypeStruct((batch, nh, HEAD_DIM, s), F32),
                   jax.ShapeDtypeStruct((batch, 2, npairs, PAIR, s), BF16)],
        compiler_params=_params("arbitrary"),
        name="qkv_prompt",
    )(x, g_all, mod, mod, w_q, w_kv_t)


def _sb_prompt_kernel(qi_ref, kj_ref, bias_ref, q_ref, kt_ref, vt_ref, o_ref, acc, tail, *, t):
    s = pl.program_id(1)
    i = qi_ref[s]
    j = kj_ref[s]
    first = j == i
    npairs = q_ref.shape[0]

    @pl.when(first)
    def _():
        acc[...] = jnp.zeros_like(acc)
        tail[...] = jnp.zeros_like(tail)

    row = lax.broadcasted_iota(jnp.int32, (t, t), 0)
    col = lax.broadcasted_iota(jnp.int32, (t, t), 1)
    tri = (row > col).astype(BF16)
    low = lax.broadcasted_iota(jnp.int32, (1, PAIR), 1) < HEAD_DIM
    top = lax.broadcasted_iota(jnp.int32, (2 * t, 1), 0) < t
    causal2 = jnp.concatenate([col < row] * 2, axis=0)

    def pair_body(hp, carry, *, diagonal):
        q2 = q_ref[hp]
        k2 = kt_ref[hp]
        v2 = vt_ref[hp]
        zero = jnp.zeros_like(q2)
        qs = jnp.concatenate([jnp.where(low, q2, zero), jnp.where(low, zero, q2)], axis=0)
        bias = jnp.where(top, bias_ref[2 * hp], bias_ref[2 * hp + 1])
        sp, lsig = _sb_logits(_dot(qs, k2) + bias)
        if diagonal:
            sp = jnp.where(causal2, sp, 0.0)
        later = _dot(sp.astype(BF16), tri)
        tl = tail[hp]
        w = jnp.exp2(lsig - later - jnp.concatenate([tl] * (t // LANES), axis=1))
        if diagonal:
            w = jnp.where(causal2, w, 0.0)
        ov = lax.dot_general(w.astype(BF16), v2, NT, preferred_element_type=F32)
        acc[hp] = acc[hp] + jnp.where(low, ov[:t], ov[t:])
        tail[hp] = tl + (later[:, 0:1] + sp[:, 0:1])
        return carry

    @pl.when(first)
    def _():
        lax.fori_loop(0, npairs, functools.partial(pair_body, diagonal=True), 0, unroll=2)

    @pl.when(jnp.logical_not(first))
    def _():
        lax.fori_loop(0, npairs, functools.partial(pair_body, diagonal=False), 0, unroll=True)

    @pl.when(j == 0)
    def _():
        o_ref[...] = acc[...].astype(BF16)


def sb_attention_prompt(q_pairs, kv_t, bias, t=256):
    batch, npairs, s, _ = q_pairs.shape
    nq = s // t
    qi, kj = [], []
    for i in range(nq):
        for j in range(i, -1, -1):
            qi.append(i)
            kj.append(j)
    qi = jnp.asarray(qi, jnp.int32)
    kj = jnp.asarray(kj, jnp.int32)
    blk = (None, npairs, t, PAIR)
    blk_t = (None, None, npairs, PAIR, t)
    grid_spec = pltpu.PrefetchScalarGridSpec(
        num_scalar_prefetch=2,
        grid=(batch, qi.shape[0]),
        in_specs=[pl.BlockSpec(memory_space=pltpu.SMEM),
                  pl.BlockSpec(blk, lambda b, s_, qi_, kj_: (b, 0, qi_[s_], 0)),
                  pl.BlockSpec(blk_t, lambda b, s_, qi_, kj_: (b, 0, 0, 0, kj_[s_])),
                  pl.BlockSpec(blk_t, lambda b, s_, qi_, kj_: (b, 1, 0, 0, kj_[s_]))],
        out_specs=pl.BlockSpec(blk, lambda b, s_, qi_, kj_: (b, 0, qi_[s_], 0)),
        scratch_shapes=[pltpu.VMEM((npairs, t, PAIR), F32), pltpu.VMEM((npairs, 2 * t, LANES), F32)],
    )
    return pl.pallas_call(
        functools.partial(_sb_prompt_kernel, t=t),
        grid_spec=grid_spec,
        out_shape=jax.ShapeDtypeStruct((batch, npairs, s, PAIR), BF16),
        compiler_params=_params("arbitrary", "arbitrary"),
        name="sb_prompt",
    )(qi, kj, bias, q_pairs, kv_t, kv_t)


def _sb_decode_kernel(pt_ref, bias_ref, qc_ref, knc_ref, vnc_ref, *rest, pages, past_len):
    k_refs = rest[:pages]
    v_refs = rest[pages:2 * pages]
    o_ref, acc, tail, zw = rest[2 * pages:]
    g = pl.program_id(1)
    nh, hd, page = acc.shape

    @pl.when(g == 0)
    def _():
        lane0 = lax.broadcasted_iota(jnp.int32, (1, page), 1) == 0
        pos = past_len + lax.broadcasted_iota(jnp.int32, (1, page), 1) * 0
        vis = pos < pos

        def init_head(h, carry):
            z = jnp.sum(qc_ref[h] * knc_ref[h], axis=0, keepdims=True) + bias_ref[h]
            sp, lsig = _sb_logits(z)
            acc[h] = jnp.where(vis & lane0, jnp.exp2(lsig), 0.0) * vnc_ref[h]
            tail[h] = jnp.broadcast_to(jnp.where(vis, sp, 0.0), (pages, page))
            return carry

        lax.fori_loop(0, nh, init_head, 0)

    def logits_body(h, carry):
        qh = qc_ref[h]
        rows = [jnp.sum(k_refs[p][h] * qh, axis=0, keepdims=True) for p in range(pages)]
        zw[h] = jnp.concatenate(rows, axis=0) + bias_ref[h]
        return carry

    lax.fori_loop(0, nh, logits_body, 0, unroll=2)

    r_ = lax.broadcasted_iota(jnp.int32, (page, page), 0)
    c_ = lax.broadcasted_iota(jnp.int32, (page, page), 1)
    tri = (r_ > c_).astype(BF16)
    slot = lax.broadcasted_iota(jnp.int32, (nh * pages, page), 0) % pages
    sp, lsig = _sb_logits(zw[...].reshape(nh * pages, page))
    within = _split_dot(sp, tri)
    rs = jnp.broadcast_to(jnp.sum(sp, axis=1, keepdims=True), sp.shape)
    run = rs
    sft = 1
    while sft < pages:
        run = run + jnp.where(slot >= sft, pltpu.roll(run, sft, 0), 0.0)
        sft *= 2
    tl = tail[...].reshape(nh * pages, page)
    zw[...] = jnp.exp2(lsig - within - (run - rs) - tl).reshape(nh, pages, page)
    total = run.reshape(nh, pages, page)[:, pages - 1:pages, :]
    tail[...] = tail[...] + jnp.broadcast_to(total, (nh, pages, page))

    def values_body(h, carry):
        w = zw[h]
        a = acc[h]
        for p in range(pages):
            a = a + w[p:p + 1, :] * v_refs[p][h]
        acc[h] = a
        return carry

    lax.fori_loop(0, nh, values_body, 0, unroll=2)

    @pl.when(g == pl.num_programs(1) - 1)
    def _():
        o_ref[...] = jnp.sum(acc[...], axis=2)


def sb_attention_decode(q, k_new, v_new, bias, cache_k, cache_v, page_table, pages=8):
    b, d = q.shape
    n_phys, page, nh, hd = cache_k.shape
    n_pages = page_table.shape[1]
    assert page == LANES and pages == 8 and n_pages % pages == 0
    ck = jnp.transpose(cache_k, (0, 2, 3, 1))
    cv = jnp.transpose(cache_v, (0, 2, 3, 1))
    blk = (None, nh, hd, page)
    seq = pl.BlockSpec(blk, lambda i, g, pt: (i, 0, 0, 0))

    def page_spec(p):
        return pl.BlockSpec(blk, lambda i, g, pt: (pt[i, n_pages - 1 - (g * pages + p)], 0, 0, 0))

    def cols(a):
        return jnp.broadcast_to(a.reshape(b, nh, hd, 1), (b, nh, hd, page))

    grid_spec = pltpu.PrefetchScalarGridSpec(
        num_scalar_prefetch=1,
        grid=(b, n_pages // pages),
        in_specs=[pl.BlockSpec(memory_space=pltpu.SMEM), seq, seq, seq]
        + [page_spec(p) for p in range(pages)] * 2,
        out_specs=pl.BlockSpec((None, nh, hd), lambda i, g, pt: (i, 0, 0)),
        scratch_shapes=[pltpu.VMEM((nh, hd, page), F32), pltpu.VMEM((nh, pages, page), F32),
                        pltpu.VMEM((nh, pages, page), F32)],
    )
    out = pl.pallas_call(
        functools.partial(_sb_decode_kernel, pages=pages, past_len=n_pages * page),
        grid_spec=grid_spec,
        out_shape=jax.ShapeDtypeStruct((b, nh, hd), F32),
        compiler_params=_params("arbitrary", "arbitrary"),
        name="sb_decode",
    )(page_table, bias.astype(F32), cols(q), cols(k_new), cols(v_new), *([ck] * pages), *([cv] * pages))
    return out.reshape(b, d)


def _proj_post_kernel(a_ref, w_ref, x_ref, g_ref, gate_ref, o_ref, *, paired):
    if paired:
        a = jnp.concatenate([a_ref[t] for t in range(a_ref.shape[0])], axis=1)
    else:
        a = a_ref[...]
    y = _dot(a.astype(BF16), w_ref[...])
    o_ref[...] = x_ref[...] + gate_ref[...] * _rms(y, g_ref[...])


def proj_post(a, w, x, g_all, g_idx, mod, gate_col, tm):
    m, d = x.shape
    paired = a.ndim == 4
    bpg = (m // tm) // mod.shape[0]
    if paired:
        spb = a.shape[2] // tm
        a_spec = pl.BlockSpec((None, a.shape[1], tm, PAIR), lambda i: (i // spb, 0, i % spb, 0))
    else:
        a_spec = pl.BlockSpec((tm, a.shape[1]), lambda i: (i, 0))
    return pl.pallas_call(
        functools.partial(_proj_post_kernel, paired=paired),
        grid=(m // tm,),
        in_specs=[a_spec, pl.BlockSpec(w.shape, lambda i: (0, 0)), pl.BlockSpec((tm, d), lambda i: (i, 0)),
                  _g_spec(d, g_idx), _mod_spec(mod, d, gate_col, bpg)],
        out_specs=pl.BlockSpec((tm, d), lambda i: (i, 0)),
        out_shape=jax.ShapeDtypeStruct((m, d), F32),
        compiler_params=_params("arbitrary"),
        name="proj_post",
    )(a, w, x, g_all, mod)


def _ffn_kernel(x_ref, g_ref, sh_ref, sc_ref, wg_ref, wu_ref, wd_ref, gp_ref, gate_ref, o_ref, h_scr, acc):
    f = pl.program_id(1)

    @pl.when(f == 0)
    def _():
        h_scr[...] = _normmod(x_ref[...], g_ref[...], sh_ref[...], sc_ref[...]).astype(BF16)
        acc[...] = jnp.zeros_like(acc)

    h = h_scr[...]
    act = _silu(_dot(h, wg_ref[...])) * _dot(h, wu_ref[...])
    acc[...] += _dot(act.astype(BF16), wd_ref[...])

    @pl.when(f == pl.num_programs(1) - 1)
    def _():
        o_ref[...] = x_ref[...] + gate_ref[...] * _rms(acc[...], gp_ref[...])


def ffn_sublayer(x, g_all, g_pre, g_post, mod, sh_col, sc_col, gate_col, w_gu, w_dn, tm, fc):
    m, d = x.shape
    dff = w_dn.shape[0]
    nf = dff // fc
    bpg = (m // tm) // mod.shape[0]
    wmode = dict(pipeline_mode=pl.Buffered(1)) if nf == 1 else {}
    return pl.pallas_call(
        _ffn_kernel,
        grid=(m // tm, nf),
        in_specs=[pl.BlockSpec((tm, d), lambda i, f: (i, 0)),
                  _g_spec(d, g_pre), _mod_spec(mod, d, sh_col, bpg), _mod_spec(mod, d, sc_col, bpg),
                  pl.BlockSpec((d, fc), lambda i, f: (0, f), **wmode),
                  pl.BlockSpec((d, fc), lambda i, f: (0, nf + f), **wmode),
                  pl.BlockSpec((fc, d), lambda i, f: (f, 0), **wmode),
                  _g_spec(d, g_post), _mod_spec(mod, d, gate_col, bpg)],
        out_specs=pl.BlockSpec((tm, d), lambda i, f: (i, 0)),
        out_shape=jax.ShapeDtypeStruct((m, d), F32),
        scratch_shapes=[pltpu.VMEM((tm, d), BF16), pltpu.VMEM((tm, d), F32)],
        compiler_params=_params("arbitrary", "arbitrary"),
        name="ffn",
    )(x, g_all, mod, mod, w_gu, w_gu, w_dn, g_all, mod)


def _expand_heads(a, e_bf16):
    return _split_dot(a, e_bf16)


def _ssd_prompt_kernel(z_ref, x_ref, bc_ref, dt_ref, cwx_ref, cwbc_ref, cbx_ref, cbbc_ref, dtb_ref, a_ref,
                       dsk_ref, ng_ref, e_ref, y_ref, st_ref, conv_ref, xs_x, xs_bc, y_scr,
                       *, chunk, groups, dstate):
    c = pl.program_id(1)
    last = c == pl.num_programs(1) - 1
    L = chunk
    d_inner = x_ref.shape[1]
    npairs = d_inner // PAIR
    ppg = npairs // groups
    kw = cwx_ref.shape[0]

    @pl.when(c == 0)
    def _():
        xs_x[0:8, :] = jnp.zeros((8, xs_x.shape[1]), F32)
        xs_bc[0:8, :] = jnp.zeros((8, xs_bc.shape[1]), F32)
        st_ref[...] = jnp.zeros_like(st_ref)

    def conv_act(cur_ref, xs, cw_ref, cb_ref):
        xs[8:8 + L, :] = cur_ref[...]
        out = cb_ref[...] + cw_ref[kw - 1:kw, :] * cur_ref[...]
        for k in range(kw - 1):
            sft = kw - 1 - k
            out = out + cw_ref[k:k + 1, :] * xs[8 - sft:8 - sft + L, :]
        return _silu(out)

    x_act = conv_act(x_ref, xs_x, cwx_ref, cbx_ref)
    bc_act = conv_act(bc_ref, xs_bc, cwbc_ref, cbbc_ref)

    @pl.when(last)
    def _():
        conv_ref[:, 0:d_inner] = xs_x[8 + L - (kw - 1):8 + L, :]
        conv_ref[:, d_inner:] = xs_bc[8 + L - (kw - 1):8 + L, :]

    xs_x[0:8, :] = xs_x[L:L + 8, :]
    xs_bc[0:8, :] = xs_bc[L:L + 8, :]

    dt = _softplus(dt_ref[...] + dtb_ref[...])
    dta = dt * a_ref[...]
    r_ = lax.broadcasted_iota(jnp.int32, (L, L), 0)
    c_ = lax.broadcasted_iota(jnp.int32, (L, L), 1)
    causal = r_ >= c_
    a_cs = _dot_hi(causal.astype(F32), dta)
    a_cs_t = a_cs.T
    a_end = a_cs[L - 1:L, :]
    e = e_ref[...]
    dt_x = _expand_heads(dt, e)
    grow_x = _expand_heads(jnp.exp(a_cs), e)
    dend_x = _expand_heads(jnp.exp(a_end - a_cs), e)
    cdec = jnp.exp(a_end)

    low = lax.broadcasted_iota(jnp.int32, (1, PAIR), 1) < HEAD_DIM
    rlow = lax.broadcasted_iota(jnp.int32, (PAIR, 1), 0) < HEAD_DIM
    gw = groups * dstate
    for g in range(groups):
        bg = bc_act[:, g * dstate:(g + 1) * dstate].astype(BF16)
        cg = bc_act[:, gw + g * dstate:gw + (g + 1) * dstate].astype(BF16)
        cb = lax.dot_general(cg, bg, NT, preferred_element_type=F32)
        for pp in range(ppg):
            hp = g * ppg + pp
            sl = slice(hp * PAIR, (hp + 1) * PAIR)
            xdt = x_act[:, sl] * dt_x[:, sl]
            xdt_b = xdt.astype(BF16)
            y = jnp.zeros((L, PAIR), F32)
            for hh in range(2):
                h = 2 * hp + hh
                seg = a_cs[:, h:h + 1] - a_cs_t[h:h + 1, :]
                dec = jnp.where(causal, jnp.exp(jnp.where(causal, seg, 0.0)), 0.0)
                sel = low if hh == 0 else jnp.logical_not(low)
                y = y + _dot((cb * dec).astype(BF16), jnp.where(sel, xdt_b, jnp.zeros_like(xdt_b)))
            prev = st_ref[hp]
            y = y + lax.dot_general(cg, prev.astype(BF16), NT, preferred_element_type=F32) * grow_x[:, sl]
            y_scr[:, sl] = y
            xw = (xdt * dend_x[:, sl]).astype(BF16)
            s_c = lax.dot_general(xw, bg, TN, preferred_element_type=F32)
            cd = jnp.where(rlow, cdec[:, 2 * hp:2 * hp + 1], cdec[:, 2 * hp + 1:2 * hp + 2])
            st_ref[hp] = prev * cd + s_c

    y = y_scr[...] + x_act * dsk_ref[...]
    y = y * _silu(z_ref[...])
    gsz = d_inner // groups
    for g in range(groups):
        sl = slice(g * gsz, (g + 1) * gsz)
        y_ref[:, sl] = _rms(y[:, sl], ng_ref[:, sl]).astype(BF16)


def _head_expander(nheads):
    lane = jnp.arange(nheads * HEAD_DIM)[None, :] // HEAD_DIM
    return (lane == jnp.arange(LANES)[:, None]).astype(BF16)


def ssd_prompt(zx, dt_raw, conv_w, conv_b, dt_bias, a_neg, d_skip, norm_g, batch, chunk, groups, dstate):
    m = zx.shape[0]
    d_inner = d_skip.shape[0] * HEAD_DIM
    nheads = d_skip.shape[0]
    bcw = 2 * groups * dstate
    s = m // batch
    nc = s // chunk
    kw = conv_w.shape[0]
    npairs = d_inner // PAIR
    pad = LANES - nheads
    dtb = jnp.pad(dt_bias.astype(F32), (0, pad)).reshape(1, LANES)
    a_p = jnp.pad(a_neg, (0, pad)).reshape(1, LANES)
    dsk = jnp.repeat(d_skip.astype(F32), HEAD_DIM).reshape(1, d_inner)
    cb2 = conv_b.reshape(1, -1)
    rb = lambda b, c: b * nc + c
    const = lambda shape, j=0: pl.BlockSpec(shape, lambda b, c: (0, j))
    return pl.pallas_call(
        functools.partial(_ssd_prompt_kernel, chunk=chunk, groups=groups, dstate=dstate),
        grid=(batch, nc),
        in_specs=[pl.BlockSpec((chunk, d_inner), lambda b, c: (rb(b, c), 0)),
                  pl.BlockSpec((chunk, d_inner), lambda b, c: (rb(b, c), 1)),
                  pl.BlockSpec((chunk, bcw), lambda b, c: (rb(b, c), 2 * d_inner // bcw)),
                  pl.BlockSpec((chunk, LANES), lambda b, c: (rb(b, c), 0)),
                  const((kw, d_inner)), const((kw, bcw), d_inner // bcw),
                  const((1, d_inner)), const((1, bcw), d_inner // bcw),
                  const((1, LANES)), const((1, LANES)), const((1, d_inner)), const((1, d_inner)),
                  const((LANES, d_inner))],
        out_specs=[pl.BlockSpec((chunk, d_inner), lambda b, c: (rb(b, c), 0)),
                   pl.BlockSpec((None, npairs, PAIR, dstate), lambda b, c: (b, 0, 0, 0)),
                   pl.BlockSpec((None, kw - 1, d_inner + bcw), lambda b, c: (b, 0, 0))],
        out_shape=[jax.ShapeDtypeStruct((m, d_inner), BF16),
                   jax.ShapeDtypeStruct((batch, npairs, PAIR, dstate), F32),
                   jax.ShapeDtypeStruct((batch, kw - 1, d_inner + bcw), F32)],
        scratch_shapes=[pltpu.VMEM((chunk + 8, d_inner), F32), pltpu.VMEM((chunk + 8, bcw), F32),
                        pltpu.VMEM((chunk, d_inner), F32)],
        compiler_params=_params("arbitrary", "arbitrary"),
        name="ssd_prompt",
    )(zx, zx, zx, dt_raw, conv_w, conv_w, cb2, cb2, dtb, a_p, dsk, norm_g.reshape(1, d_inner),
      _head_expander(nheads))


def _ssd_step_kernel(z_ref, x_ref, bc_ref, dt_ref, sx_ref, sbc_ref, st_ref, cwx_ref, cwbc_ref, cbx_ref, cbbc_ref,
                     dtb_ref, a_ref, dsk_ref, ng_ref, y_ref, sto_ref, conv_ref, y_scr, *, groups, dstate):
    d_inner = x_ref.shape[1]
    npairs = d_inner // PAIR
    ppg = npairs // groups
    kw = cwx_ref.shape[0]

    def conv_act(new_ref, s_ref, cw_ref, cb_ref):
        out = cb_ref[...] + cw_ref[kw - 1:kw, :] * new_ref[...]
        for k in range(kw - 1):
            out = out + cw_ref[k:k + 1, :] * s_ref[k:k + 1, :]
        return _silu(out)

    x_act = conv_act(x_ref, sx_ref, cwx_ref, cbx_ref)
    bc_act = conv_act(bc_ref, sbc_ref, cwbc_ref, cbbc_ref)
    conv_ref[0:kw - 2, 0:d_inner] = sx_ref[1:kw - 1, :]
    conv_ref[0:kw - 2, d_inner:] = sbc_ref[1:kw - 1, :]
    conv_ref[kw - 2:kw - 1, 0:d_inner] = x_ref[...]
    conv_ref[kw - 2:kw - 1, d_inner:] = bc_ref[...]

    dt = _softplus(dt_ref[...] + dtb_ref[...])
    decay = jnp.exp(dt * a_ref[...])
    rlow = lax.broadcasted_iota(jnp.int32, (PAIR, 1), 0) < HEAD_DIM
    low = lax.broadcasted_iota(jnp.int32, (1, PAIR), 1) < HEAD_DIM
    eye = lax.broadcasted_iota(jnp.int32, (PAIR, PAIR), 0) == lax.broadcasted_iota(jnp.int32, (PAIR, PAIR), 1)
    gw = groups * dstate
    for hp in range(npairs):
        g = hp // ppg
        sl = slice(hp * PAIR, (hp + 1) * PAIR)
        bg = jnp.broadcast_to(bc_act[:, g * dstate:(g + 1) * dstate], (PAIR, dstate))
        cg = jnp.broadcast_to(bc_act[:, gw + g * dstate:gw + (g + 1) * dstate], (8, dstate))
        dt_p = jnp.where(low, dt[:, 2 * hp:2 * hp + 1], dt[:, 2 * hp + 1:2 * hp + 2])
        xdt = x_act[:, sl] * dt_p
        outer = _dot_hi(jnp.where(eye, jnp.broadcast_to(xdt, (PAIR, PAIR)), 0.0), bg)
        dc = jnp.where(rlow, decay[:, 2 * hp:2 * hp + 1], decay[:, 2 * hp + 1:2 * hp + 2])
        new = st_ref[hp] * dc + outer
        sto_ref[hp] = new
        y_scr[:, sl] = _dot_hi(cg, new, NT)[0:1, :]

    y = y_scr[...] + x_act * dsk_ref[...]
    y = y * _silu(z_ref[...])
    gsz = d_inner // groups
    for g in range(groups):
        sl = slice(g * gsz, (g + 1) * gsz)
        y_ref[:, sl] = _rms(y[:, sl], ng_ref[:, sl]).astype(BF16)


def ssd_step(zx, dt_raw, state_conv, state_ssm, conv_w, conv_b, dt_bias, a_neg, d_skip, norm_g, groups, dstate):
    b = zx.shape[0]
    nheads = d_skip.shape[0]
    d_inner = nheads * HEAD_DIM
    bcw = 2 * groups * dstate
    kw = conv_w.shape[0]
    npairs = d_inner // PAIR
    pad = LANES - nheads
    dtb = jnp.pad(dt_bias.astype(F32), (0, pad)).reshape(1, LANES)
    a_p = jnp.pad(a_neg, (0, pad)).reshape(1, LANES)
    dsk = jnp.repeat(d_skip.astype(F32), HEAD_DIM).reshape(1, d_inner)
    cb2 = conv_b.reshape(1, -1)
    zx3 = zx.reshape(b, 1, -1)
    st = state_ssm.reshape(b, npairs, PAIR, dstate)
    const = lambda shape, j=0: pl.BlockSpec(shape, lambda i: (0, j))
    y, st_new, conv_new = pl.pallas_call(
        functools.partial(_ssd_step_kernel, groups=groups, dstate=dstate),
        grid=(b,),
        in_specs=[pl.BlockSpec((None, 1, d_inner), lambda i: (i, 0, 0)),
                  pl.BlockSpec((None, 1, d_inner), lambda i: (i, 0, 1)),
                  pl.BlockSpec((None, 1, bcw), lambda i: (i, 0, 2 * d_inner // bcw)),
                  pl.BlockSpec((None, 1, LANES), lambda i: (i, 0, 0)),
                  pl.BlockSpec((None, kw - 1, d_inner), lambda i: (i, 0, 0)),
                  pl.BlockSpec((None, kw - 1, bcw), lambda i: (i, 0, d_inner // bcw)),
                  pl.BlockSpec((None, npairs, PAIR, dstate), lambda i: (i, 0, 0, 0)),
                  const((kw, d_inner)), const((kw, bcw), d_inner // bcw),
                  const((1, d_inner)), const((1, bcw), d_inner // bcw),
                  const((1, LANES)), const((1, LANES)), const((1, d_inner)), const((1, d_inner))],
        out_specs=[pl.BlockSpec((None, 1, d_inner), lambda i: (i, 0, 0)),
                   pl.BlockSpec((None, npairs, PAIR, dstate), lambda i: (i, 0, 0, 0)),
                   pl.BlockSpec((None, kw - 1, d_inner + bcw), lambda i: (i, 0, 0))],
        out_shape=[jax.ShapeDtypeStruct((b, 1, d_inner), BF16),
                   jax.ShapeDtypeStruct((b, npairs, PAIR, dstate), F32),
                   jax.ShapeDtypeStruct((b, kw - 1, d_inner + bcw), F32)],
        scratch_shapes=[pltpu.VMEM((1, d_inner), F32)],
        compiler_params=_params("arbitrary"),
        name="ssd_step",
    )(zx3, zx3, zx3, dt_raw.reshape(b, 1, LANES), state_conv, state_conv, st, conv_w, conv_w, cb2, cb2,
      dtb, a_p, dsk, norm_g.reshape(1, d_inner))
    return y.reshape(b, d_inner), st_new, conv_new


def _store_token_rows(ref, val):
    n, d = val.shape
    r = d // LANES
    for c in range(r):
        ref[pl.ds(c, n, stride=r), :] = val[:, c * LANES:(c + 1) * LANES]


def _load_token_rows(ref, n, lead=()):
    r = ref.shape[-2] // n
    return jnp.concatenate([ref[(*lead, pl.ds(c, n, stride=r), slice(None))] for c in range(r)], axis=1)


def _token_rows(ref, t, r):
    return ref.at[pl.ds(pl.multiple_of(t * r, r), r)]


def _router_kernel(x_ref, g_ref, sh_ref, sc_ref, wr_ref, h_ref, route_ref, *, n_experts):
    h = _normmod(x_ref[...], g_ref[...], sh_ref[...], sc_ref[...])
    _store_token_rows(h_ref, h)
    logits = _dot_hi(h, wr_ref[...])
    lane = lax.broadcasted_iota(jnp.int32, logits.shape, 1).astype(F32)
    neg = jnp.float32(-jnp.inf)
    lg = jnp.where(lane < n_experts, logits, neg)
    m1 = jnp.max(lg, axis=1, keepdims=True)
    i1 = jnp.min(jnp.where(lg == m1, lane, float(LANES)), axis=1, keepdims=True)
    lg2 = jnp.where(lane == i1, neg, lg)
    m2 = jnp.max(lg2, axis=1, keepdims=True)
    i2 = jnp.min(jnp.where(lg2 == m2, lane, float(LANES)), axis=1, keepdims=True)
    e2 = jnp.exp(m2 - m1)
    g1 = 1.0 / (1.0 + e2)
    route_ref[...] = (jnp.where(lane == 0.0, i1, 0.0) + jnp.where(lane == 1.0, i2, 0.0)
                      + jnp.where(lane == 2.0, g1, 0.0) + jnp.where(lane == 3.0, e2 * g1, 0.0))


def moe_router(x, g_all, g_idx, mod, sh_col, sc_col, w_router, tm):
    m, d = x.shape
    n_experts = w_router.shape[1]
    bpg = (m // tm) // mod.shape[0]
    wr = jnp.pad(w_router.astype(F32), ((0, 0), (0, LANES - n_experts)))
    return pl.pallas_call(
        functools.partial(_router_kernel, n_experts=n_experts),
        grid=(m // tm,),
        in_specs=[pl.BlockSpec((tm, d), lambda i: (i, 0)),
                  _g_spec(d, g_idx), _mod_spec(mod, d, sh_col, bpg), _mod_spec(mod, d, sc_col, bpg),
                  pl.BlockSpec((d, LANES), lambda i: (0, 0))],
        out_specs=[pl.BlockSpec((tm * d // LANES, LANES), lambda i: (i, 0)),
                   pl.BlockSpec((tm, LANES), lambda i: (i, 0))],
        out_shape=[jax.ShapeDtypeStruct((m * d // LANES, LANES), F32), jax.ShapeDtypeStruct((m, LANES), F32)],
        compiler_params=_params("arbitrary"),
        name="moe_router",
    )(x, g_all, mod, mod, wr)


def moe_plan(routes, n_experts, tm):
    t = routes.shape[0]
    e = routes[:, :2].astype(jnp.int32).reshape(-1)
    onehot = (e[:, None] == jnp.arange(n_experts, dtype=jnp.int32)[None, :]).astype(jnp.int32)
    before = jnp.cumsum(onehot, axis=0) - onehot
    counts = jnp.sum(onehot, axis=0)
    padded = (counts + tm - 1) // tm * tm
    ends = jnp.cumsum(padded)
    pos = jnp.sum(onehot * (before + (ends - padded)[None, :]), axis=1)
    nb = -(-(2 * t + n_experts * (tm - 1)) // tm)
    blk_start = jnp.arange(nb, dtype=jnp.int32) * tm
    blk_e = jnp.minimum(jnp.sum((blk_start[:, None] >= ends[None, :]).astype(jnp.int32), axis=1), n_experts - 1)
    live = (blk_start < ends[-1]).astype(jnp.int32)
    blk_e = jnp.where(live == 1, blk_e, blk_e[jnp.maximum(jnp.sum(live) - 1, 0)])
    return pos.astype(jnp.int32), blk_e.astype(jnp.int32), live


def _slot_spec(tb, copies):
    return pl.BlockSpec((None, 1, copies * tb), lambda i, *_: (i, 0, 0), memory_space=pltpu.SMEM)


def _dispatch_kernel(pos_ref, h_ref, buf_in_ref, buf_ref, sem, *, copies, r):
    del buf_in_ref
    n = h_ref.shape[0] // r

    def row_copy(t, c):
        return pltpu.make_async_copy(_token_rows(h_ref, t, r), _token_rows(buf_ref, pos_ref[0, copies * t + c], r), sem)

    def start(t, carry):
        for c in range(copies):
            row_copy(t, c).start()
        return carry

    def wait(t, carry):
        for c in range(copies):
            row_copy(t, c).wait()
        return carry

    lax.fori_loop(0, n, start, 0, unroll=4)
    lax.fori_loop(0, n, wait, 0, unroll=4)


def moe_dispatch(h, pos, buf, tb, r, copies=2):
    m = h.shape[0] // r
    return pl.pallas_call(
        functools.partial(_dispatch_kernel, copies=copies, r=r),
        grid=(m // tb,),
        in_specs=[_slot_spec(tb, copies), pl.BlockSpec((tb * r, LANES), lambda i: (i, 0)),
                  pl.BlockSpec(memory_space=pl.ANY)],
        out_specs=pl.BlockSpec(memory_space=pl.ANY),
        out_shape=jax.ShapeDtypeStruct(buf.shape, buf.dtype),
        scratch_shapes=[pltpu.SemaphoreType.DMA(())],
        input_output_aliases={2: 0},
        compiler_params=_params("arbitrary"),
        name="moe_dispatch",
    )(pos.reshape(m // tb, 1, copies * tb), h, buf)


def _moe_experts_kernel(be_ref, live_ref, h_ref, wg_ref, wu_ref, wd_ref, y_ref, hb, acc):
    f = pl.program_id(1)
    live = live_ref[pl.program_id(0)] == 1

    @pl.when(live & (f == 0))
    def _():
        hb[...] = _load_token_rows(h_ref, hb.shape[0]).astype(BF16)
        acc[...] = jnp.zeros_like(acc)

    @pl.when(live)
    def _():
        h = hb[...]
        act = _silu(_dot(h, wg_ref[...])) * _dot(h, wu_ref[...])
        acc[...] += _dot(act.astype(BF16), wd_ref[...])

    @pl.when(f == pl.num_programs(1) - 1)
    def _():
        _store_token_rows(y_ref, jnp.where(live, acc[...], 0.0))


def moe_experts(h_sorted, blk_e, live, w_gu, w_dn, tm, fc):
    d, dff = w_dn.shape[2], w_dn.shape[1]
    r = d // LANES
    p = h_sorted.shape[0] // r
    nf = dff // fc

    def fe(b, f, live_):
        return jnp.where(live_[b] == 1, f, nf - 1)

    grid_spec = pltpu.PrefetchScalarGridSpec(
        num_scalar_prefetch=2,
        grid=(p // tm, nf),
        in_specs=[pl.BlockSpec((tm * r, LANES), lambda b, f, be, lv: (b, 0)),
                  pl.BlockSpec((None, d, fc), lambda b, f, be, lv: (be[b], 0, fe(b, f, lv))),
                  pl.BlockSpec((None, d, fc), lambda b, f, be, lv: (be[b], 0, nf + fe(b, f, lv))),
                  pl.BlockSpec((None, fc, d), lambda b, f, be, lv: (be[b], fe(b, f, lv), 0))],
        out_specs=pl.BlockSpec((tm * r, LANES), lambda b, f, be, lv: (b, 0)),
        scratch_shapes=[pltpu.VMEM((tm, d), BF16), pltpu.VMEM((tm, d), F32)],
    )
    return pl.pallas_call(
        _moe_experts_kernel,
        grid_spec=grid_spec,
        out_shape=jax.ShapeDtypeStruct((p * r, LANES), F32),
        compiler_params=_params("arbitrary", "arbitrary"),
        name="moe_experts",
    )(blk_e, live, h_sorted, w_gu, w_gu, w_dn)


def _moe_combine_kernel(pos_ref, y_hbm, route_ref, x_ref, gp_ref, gate_ref, o_ref, ybuf, sem):
    tb, d = x_ref.shape
    r = d // LANES

    def row_copy(t, c):
        return pltpu.make_async_copy(_token_rows(y_hbm, pos_ref[0, 2 * t + c], r), _token_rows(ybuf.at[c], t, r), sem)

    def start(t, carry):
        for c in range(2):
            row_copy(t, c).start()
        return carry

    def wait(t, carry):
        for c in range(2):
            row_copy(t, c).wait()
        return carry

    lax.fori_loop(0, tb, start, 0, unroll=4)
    lax.fori_loop(0, tb, wait, 0, unroll=4)
    rec = route_ref[...]
    lane = lax.broadcasted_iota(jnp.int32, rec.shape, 1)
    g1 = jnp.sum(jnp.where(lane == 2, rec, 0.0), axis=1, keepdims=True)
    g2 = jnp.sum(jnp.where(lane == 3, rec, 0.0), axis=1, keepdims=True)
    y = g1 * _load_token_rows(ybuf, tb, (0,)) + g2 * _load_token_rows(ybuf, tb, (1,))
    o_ref[...] = x_ref[...] + gate_ref[...] * _rms(y, gp_ref[...])


def moe_combine(y_sorted, pos, routes, x, g_all, g_post, mod, gate_col, tb):
    m, d = x.shape
    bpg = (m // tb) // mod.shape[0]
    return pl.pallas_call(
        _moe_combine_kernel,
        grid=(m // tb,),
        in_specs=[_slot_spec(tb, 2), pl.BlockSpec(memory_space=pl.ANY),
                  pl.BlockSpec((tb, LANES), lambda i: (i, 0)), pl.BlockSpec((tb, d), lambda i: (i, 0)),
                  _g_spec(d, g_post), _mod_spec(mod, d, gate_col, bpg)],
        out_specs=pl.BlockSpec((tb, d), lambda i: (i, 0)),
        out_shape=jax.ShapeDtypeStruct((m, d), F32),
        scratch_shapes=[pltpu.VMEM((2, tb * d // LANES, LANES), F32), pltpu.SemaphoreType.DMA(())],
        compiler_params=_params("arbitrary"),
        name="moe_combine",
    )(pos.reshape(m // tb, 1, 2 * tb), y_sorted, routes, x, g_all, mod)


def kernel(x_prompt, x_sample, cache_k, cache_v, state_ssm, state_conv, page_table, c_prompt, c_sample, norm_g, w_ada, b_ada, sb_w_qkv, sb_w_o, sb_logit_bias, ssd_w_in, ssd_conv_w, ssd_conv_b, ssd_dt_bias, ssd_a_log, ssd_d, ssd_norm_g, ssd_w_out, ffn_w_gu, ffn_w_dn, moe_w_router, moe_w_gu, moe_w_dn):
    bp, seq, d = x_prompt.shape
    bs = x_sample.shape[0]
    nh_sb = sb_logit_bias.shape[0]
    nh_ssd = ssd_d.shape[0]
    d_inner = nh_ssd * HEAD_DIM
    dstate = state_ssm.shape[-1]
    groups = (ssd_conv_w.shape[1] - d_inner) // (2 * dstate)
    n_zx = d_inner + ssd_conv_w.shape[1]
    mp = bp * seq
    tm = min(512, seq)
    tm_big = min(1024, seq)

    c_all = jnp.concatenate([c_prompt, c_sample], axis=0)
    c_all = jnp.pad(c_all, ((0, -c_all.shape[0] % 8), (0, 0)))
    ada = adaln_all(c_all, w_ada, b_ada)
    g_all = norm_g.reshape(-1, 1, d)

    log2e = 1.4426950408889634
    qscale = jnp.concatenate([jnp.full((d,), log2e * HEAD_DIM ** -0.5, F32), jnp.ones((2 * d,), F32)])
    sb_bias = sb_logit_bias.astype(F32) * log2e
    w_qkv = (sb_w_qkv * qscale).astype(BF16)
    w_o = sb_w_o.astype(BF16)
    w_gu = ffn_w_gu.astype(BF16)
    w_dn = ffn_w_dn.astype(BF16)
    w_in = ssd_w_in[:, :n_zx].astype(BF16)
    w_dt = jnp.pad(ssd_w_in[:, n_zx:], ((0, 0), (0, LANES - nh_ssd))).astype(BF16)
    w_out = ssd_w_out.astype(BF16)
    mw_gu = moe_w_gu.astype(BF16)
    mw_dn = moe_w_dn.astype(BF16)
    a_neg = -jnp.exp(ssd_a_log.astype(F32))

    xp = x_prompt.reshape(mp, d)
    xs = x_sample.reshape(bs, d)

    mod_p = ada[0, :bp].reshape(bp, 1, 6 * d)
    mod_s = ada[0, bp:bp + bs].reshape(1, bs, 6 * d)
    q_pairs, k_t, v_t, kv_t = qkv_prompt(xp, g_all, 0, mod_p, 0, 1, w_qkv[:, :d], w_qkv[:, d:].T, bp, tm)
    o_p = sb_attention_prompt(q_pairs, kv_t, sb_bias, t=min(256, seq))
    xp = proj_post(o_p, w_o, xp, g_all, 1, mod_p, 2, tm)
    xp = ffn_sublayer(xp, g_all, 2, 3, mod_p, 3, 4, 5, w_gu, w_dn, tm, w_dn.shape[0])

    qkv_s = nm_matmul(xs, g_all, 0, mod_s, 0, 1, w_qkv, bs, 512)
    q_s, k_s, v_s = qkv_s[:, :d], qkv_s[:, d:2 * d], qkv_s[:, 2 * d:]
    o_s = sb_attention_decode(q_s, k_s, v_s, sb_bias, cache_k, cache_v, page_table)
    xs = proj_post(o_s, w_o, xs, g_all, 1, mod_s, 2, bs)
    xs = ffn_sublayer(xs, g_all, 2, 3, mod_s, 3, 4, 5, w_gu, w_dn, bs, w_dn.shape[0] // 2)

    mod_p = ada[1, :bp].reshape(bp, 1, 6 * d)
    mod_s = ada[1, bp:bp + bs].reshape(1, bs, 6 * d)
    zx_p, dt_p = nm_matmul(xp, g_all, 4, mod_p, 0, 1, w_in, tm_big, n_zx // 4, extra_w=(w_dt,))
    y_p, ssm_p, conv_p = ssd_prompt(zx_p, dt_p, ssd_conv_w, ssd_conv_b, ssd_dt_bias, a_neg, ssd_d, ssd_norm_g,
                                    bp, min(128, seq), groups, dstate)
    xp = proj_post(y_p, w_out, xp, g_all, 5, mod_p, 2, tm)

    zx_s, dt_s = nm_matmul(xs, g_all, 4, mod_s, 0, 1, w_in, bs, 512, extra_w=(w_dt,))
    y_s, ssm_s, conv_s = ssd_step(zx_s, dt_s, state_conv, state_ssm, ssd_conv_w, ssd_conv_b, ssd_dt_bias, a_neg,
                                  ssd_d, ssd_norm_g, groups, dstate)
    xs = proj_post(y_s, w_out, xs, g_all, 5, mod_s, 2, bs)

    n_experts = moe_w_router.shape[1]
    tb = min(256, seq)
    h_p, route_p = moe_router(xp, g_all, 6, mod_p, 3, 4, moe_w_router, tm)
    h_s, route_s = moe_router(xs, g_all, 6, mod_s, 3, 4, moe_w_router, bs)
    pos, blk_e, live = moe_plan(jnp.concatenate([route_p, route_s], axis=0), n_experts, tm)
    rpt = d // LANES
    h_sorted = jnp.zeros((blk_e.shape[0] * tm * rpt, LANES), F32)
    h_sorted = moe_dispatch(h_p, pos[:2 * mp], h_sorted, tb, rpt)
    h_sorted = moe_dispatch(h_s, pos[2 * mp:], h_sorted, bs, rpt)
    y_sorted = moe_experts(h_sorted, blk_e, live, mw_gu, mw_dn, tm, mw_dn.shape[1] // 2)
    xp = moe_combine(y_sorted, pos[:2 * mp], route_p, xp, g_all, 7, mod_p, 5, tb)
    xs = moe_combine(y_sorted, pos[2 * mp:], route_s, xs, g_all, 7, mod_s, 5, bs)

    hd = (nh_sb, d // nh_sb)
    return (xp.reshape(bp, seq, d), xs.reshape(bs, 1, d),
            jnp.transpose(k_t, (0, 3, 1, 2)), jnp.transpose(v_t, (0, 3, 1, 2)),
            k_s.reshape(bs, 1, *hd), v_s.reshape(bs, 1, *hd),
            ssm_p.reshape(bp, nh_ssd, HEAD_DIM, dstate), conv_p,
            ssm_s.reshape(bs, nh_ssd, HEAD_DIM, dstate), conv_s)
```

```python
import functools

import jax
import jax.numpy as jnp
from jax import lax
from jax.experimental import pallas as pl
from jax.experimental.pallas import tpu as pltpu

F32 = jnp.float32
BF16 = jnp.bfloat16
EPS = 1e-6
LANES = 128
HEAD_DIM = 64
PAIR = 2 * HEAD_DIM
VMEM_LIMIT = 56 * 1024 * 1024

NT = (((1,), (1,)), ((), ()))
TN = (((0,), (0,)), ((), ()))


def _params(*sem):
    return pltpu.CompilerParams(dimension_semantics=sem, vmem_limit_bytes=VMEM_LIMIT)


def _silu(x):
    return (0.5 * x) * (1.0 + jnp.tanh(0.5 * x))


def _softplus(z):
    return jnp.maximum(z, 0.0) + jnp.log(1.0 + jnp.exp(-jnp.abs(z)))


def _sb_logits(z):
    sp = jnp.maximum(z, 0.0) + jnp.log2(1.0 + jnp.exp2(jnp.minimum(z, -z)))
    return sp, z - sp


def _rms(x, g):
    return x * lax.rsqrt(jnp.mean(x * x, axis=-1, keepdims=True) + EPS) * g


def _normmod(x, g, shift, scale):
    return _rms(x, g) * (1.0 + scale) + shift


def _dot(a, b):
    return jnp.dot(a, b, preferred_element_type=F32)


def _dot_hi(a, b, dims=None):
    if dims is None:
        return jnp.dot(a, b, preferred_element_type=F32, precision=lax.Precision.HIGHEST)
    return lax.dot_general(a, b, dims, preferred_element_type=F32, precision=lax.Precision.HIGHEST)


def _split_dot(a, b_bf16):
    hi = a.astype(BF16)
    lo = (a - hi.astype(F32)).astype(BF16)
    return _dot(hi, b_bf16) + _dot(lo, b_bf16)


def _mod_spec(mod, d, col, bpg):
    r = mod.shape[1]
    return pl.BlockSpec((None, r, d), lambda i, *_: (i // bpg, 0, col))


def _g_spec(d, idx):
    return pl.BlockSpec((None, 1, d), lambda i, *_: (idx, 0, 0))


def _adaln_kernel(c_ref, w_ref, b_ref, o_ref):
    s = _silu(c_ref[...]).astype(BF16)
    o_ref[...] = _dot(s, w_ref[...].astype(BF16)) + b_ref[...]


def adaln_all(c, w_ada, b_ada, tn=1024):
    depth, d, n = w_ada.shape
    m = c.shape[0]
    return pl.pallas_call(
        _adaln_kernel,
        grid=(depth, n // tn),
        in_specs=[pl.BlockSpec((m, d), lambda l, j: (0, 0)),
                  pl.BlockSpec((None, d, tn), lambda l, j: (l, 0, j)),
                  pl.BlockSpec((None, 1, tn), lambda l, j: (l, 0, j))],
        out_specs=pl.BlockSpec((None, m, tn), lambda l, j: (l, 0, j)),
        out_shape=jax.ShapeDtypeStruct((depth, m, n), F32),
        compiler_params=_params("arbitrary", "arbitrary"),
        name="adaln",
    )(c, w_ada, b_ada.reshape(depth, 1, n))


def _nm_matmul_kernel(x_ref, g_ref, sh_ref, sc_ref, w_ref, *rest, n_extra):
    extra_w = rest[:n_extra]
    o_ref = rest[n_extra]
    extra_o = rest[n_extra + 1:2 * n_extra + 1]
    h_scr = rest[-1]

    @pl.when(pl.program_id(1) == 0)
    def _():
        h_scr[...] = _normmod(x_ref[...], g_ref[...], sh_ref[...], sc_ref[...]).astype(BF16)
        for w2, o2 in zip(extra_w, extra_o):
            o2[...] = _dot(h_scr[...], w2[...])

    o_ref[...] = _dot(h_scr[...], w_ref[...])


def nm_matmul(x, g_all, g_idx, mod, sh_col, sc_col, w, tm, tn, extra_w=()):
    m, d = x.shape
    n = w.shape[1]
    bpg = (m // tm) // mod.shape[0]
    in_specs = [pl.BlockSpec((tm, d), lambda i, j: (i, 0)),
                _g_spec(d, g_idx), _mod_spec(mod, d, sh_col, bpg), _mod_spec(mod, d, sc_col, bpg),
                pl.BlockSpec((d, tn), lambda i, j: (0, j),
                             **(dict(pipeline_mode=pl.Buffered(1)) if tn == n else {}))]
    out_specs = [pl.BlockSpec((tm, tn), lambda i, j: (i, j))]
    out_shape = [jax.ShapeDtypeStruct((m, n), F32)]
    for w2 in extra_w:
        in_specs.append(pl.BlockSpec(w2.shape, lambda i, j: (0, 0)))
        out_specs.append(pl.BlockSpec((tm, w2.shape[1]), lambda i, j: (i, 0)))
        out_shape.append(jax.ShapeDtypeStruct((m, w2.shape[1]), F32))
    outs = pl.pallas_call(
        functools.partial(_nm_matmul_kernel, n_extra=len(extra_w)),
        grid=(m // tm, n // tn),
        in_specs=in_specs, out_specs=out_specs, out_shape=out_shape,
        scratch_shapes=[pltpu.VMEM((tm, d), BF16)],
        compiler_params=_params("arbitrary", "arbitrary"),
        name="nm_matmul",
    )(x, g_all, mod, mod, w, *extra_w)
    return outs if extra_w else outs[0]


def _qkv_kernel(x_ref, g_ref, sh_ref, sc_ref, wq_ref, wkv_ref, qp_ref, kt_ref, vt_ref, kvb_ref):
    d = x_ref.shape[1]
    npairs = d // PAIR
    h = _normmod(x_ref[...], g_ref[...], sh_ref[...], sc_ref[...]).astype(BF16)
    r = _dot(h, wq_ref[...])
    for t in range(npairs):
        qp_ref[t] = r[:, t * PAIR:(t + 1) * PAIR].astype(BF16)
    rt = lax.dot_general(wkv_ref[...], h, NT, preferred_element_type=F32)
    for t in range(d // HEAD_DIM):
        kt_ref[t] = rt[t * HEAD_DIM:(t + 1) * HEAD_DIM, :]
        vt_ref[t] = rt[d + t * HEAD_DIM:d + (t + 1) * HEAD_DIM, :]
    for t in range(2 * npairs):
        kvb_ref[t // npairs, t % npairs] = rt[t * PAIR:(t + 1) * PAIR, :].astype(BF16)


def qkv_prompt(x, g_all, g_idx, mod, sh_col, sc_col, w_q, w_kv_t, batch, tm):
    m, d = x.shape
    s = m // batch
    bpg = s // tm
    nh = d // HEAD_DIM
    npairs = d // PAIR
    return pl.pallas_call(
        _qkv_kernel,
        grid=(m // tm,),
        in_specs=[pl.BlockSpec((tm, d), lambda i: (i, 0)),
                  _g_spec(d, g_idx), _mod_spec(mod, d, sh_col, bpg), _mod_spec(mod, d, sc_col, bpg),
                  pl.BlockSpec((d, d), lambda i: (0, 0)),
                  pl.BlockSpec((2 * d, d), lambda i: (0, 0))],
        out_specs=[pl.BlockSpec((None, npairs, tm, PAIR), lambda i: (i // bpg, 0, i % bpg, 0)),
                   pl.BlockSpec((None, nh, HEAD_DIM, tm), lambda i: (i // bpg, 0, 0, i % bpg)),
                   pl.BlockSpec((None, nh, HEAD_DIM, tm), lambda i: (i // bpg, 0, 0, i % bpg)),
                   pl.BlockSpec((None, 2, npairs, PAIR, tm), lambda i: (i // bpg, 0, 0, 0, i % bpg))],
        out_shape=[jax.ShapeDtypeStruct((batch, npairs, s, PAIR), BF16),
                   jax.ShapeDtypeStruct((batch, nh, HEAD_DIM, s), F32),
                   jax.ShapeDtypeStruct((batch, nh, HEAD_DIM, s), F32),
                   jax.ShapeDtypeStruct((batch, 2, npairs, PAIR, s), BF16)],
        compiler_params=_params("arbitrary"),
        name="qkv_prompt",
    )(x, g_all, mod, mod, w_q, w_kv_t)


def _sb_prompt_kernel(qi_ref, kj_ref, bias_ref, q_ref, kt_ref, vt_ref, o_ref, acc, tail, *, t):
    s = pl.program_id(1)
    i = qi_ref[s]
    j = kj_ref[s]
    first = j == i
    npairs = q_ref.shape[0]

    @pl.when(first)
    def _():
        acc[...] = jnp.zeros_like(acc)
        tail[...] = jnp.zeros_like(tail)

    row = lax.broadcasted_iota(jnp.int32, (t, t), 0)
    col = lax.broadcasted_iota(jnp.int32, (t, t), 1)
    tri = (row > col).astype(BF16)
    low = lax.broadcasted_iota(jnp.int32, (1, PAIR), 1) < HEAD_DIM
    top = lax.broadcasted_iota(jnp.int32, (2 * t, 1), 0) < t
    causal2 = jnp.concatenate([col < row] * 2, axis=0)

    def pair_body(hp, carry, *, diagonal):
        q2 = q_ref[hp]
        k2 = kt_ref[hp]
        v2 = vt_ref[hp]
        zero = jnp.zeros_like(q2)
        qs = jnp.concatenate([jnp.where(low, q2, zero), jnp.where(low, zero, q2)], axis=0)
        bias = jnp.where(top, bias_ref[2 * hp], bias_ref[2 * hp + 1])
        sp, lsig = _sb_logits(_dot(qs, k2) + bias)
        if diagonal:
            sp = jnp.where(causal2, sp, 0.0)
        later = _dot(sp.astype(BF16), tri)
        tl = tail[hp]
        w = jnp.exp2(lsig - later - jnp.concatenate([tl] * (t // LANES), axis=1))
        if diagonal:
            w = jnp.where(causal2, w, 0.0)
        ov = lax.dot_general(w.astype(BF16), v2, NT, preferred_element_type=F32)
        acc[hp] = acc[hp] + jnp.where(low, ov[:t], ov[t:])
        tail[hp] = tl + (later[:, 0:1] + sp[:, 0:1])
        return carry

    def stacked_logits(hp):
        q2 = q_ref[hp]
        zero = jnp.zeros_like(q2)
        qs = jnp.concatenate([jnp.where(low, q2, zero), jnp.where(low, zero, q2)], axis=0)
        bias = jnp.where(top, bias_ref[2 * hp], bias_ref[2 * hp + 1])
        return _sb_logits(_dot(qs, kt_ref[hp]) + bias)

    def quad_body(hq, carry):
        parts = [stacked_logits(2 * hq + u) for u in range(2)]
        later4 = _dot(jnp.concatenate([sp for sp, _ in parts], axis=0).astype(BF16), tri)
        for u, (sp, lsig) in enumerate(parts):
            hp = 2 * hq + u
            later = later4[u * 2 * t:(u + 1) * 2 * t]
            tl = tail[hp]
            w = jnp.exp2(lsig - later - jnp.concatenate([tl] * (t // LANES), axis=1))
            ov = lax.dot_general(w.astype(BF16), vt_ref[hp], NT, preferred_element_type=F32)
            acc[hp] = acc[hp] + jnp.where(low, ov[:t], ov[t:])
            tail[hp] = tl + (later[:, 0:1] + sp[:, 0:1])
        return carry

    @pl.when(first)
    def _():
        lax.fori_loop(0, npairs, functools.partial(pair_body, diagonal=True), 0, unroll=2)

    @pl.when(jnp.logical_not(first))
    def _():
        lax.fori_loop(0, npairs // 2, quad_body, 0, unroll=True)

    @pl.when(j == 0)
    def _():
        o_ref[...] = acc[...].astype(BF16)


def sb_attention_prompt(q_pairs, kv_t, bias, t=256):
    batch, npairs, s, _ = q_pairs.shape
    nq = s // t
    qi, kj = [], []
    for i in range(nq):
        for j in range(i, -1, -1):
            qi.append(i)
            kj.append(j)
    qi = jnp.asarray(qi, jnp.int32)
    kj = jnp.asarray(kj, jnp.int32)
    blk = (None, npairs, t, PAIR)
    blk_t = (None, None, npairs, PAIR, t)
    grid_spec = pltpu.PrefetchScalarGridSpec(
        num_scalar_prefetch=2,
        grid=(batch, qi.shape[0]),
        in_specs=[pl.BlockSpec(memory_space=pltpu.SMEM),
                  pl.BlockSpec(blk, lambda b, s_, qi_, kj_: (b, 0, qi_[s_], 0)),
                  pl.BlockSpec(blk_t, lambda b, s_, qi_, kj_: (b, 0, 0, 0, kj_[s_])),
                  pl.BlockSpec(blk_t, lambda b, s_, qi_, kj_: (b, 1, 0, 0, kj_[s_]))],
        out_specs=pl.BlockSpec(blk, lambda b, s_, qi_, kj_: (b, 0, qi_[s_], 0)),
        scratch_shapes=[pltpu.VMEM((npairs, t, PAIR), F32), pltpu.VMEM((npairs, 2 * t, LANES), F32)],
    )
    return pl.pallas_call(
        functools.partial(_sb_prompt_kernel, t=t),
        grid_spec=grid_spec,
        out_shape=jax.ShapeDtypeStruct((batch, npairs, s, PAIR), BF16),
        compiler_params=_params("arbitrary", "arbitrary"),
        name="sb_prompt",
    )(qi, kj, bias, q_pairs, kv_t, kv_t)


def _sb_decode_kernel(pt_ref, bias_ref, qc_ref, knc_ref, vnc_ref, *rest, pages, past_len):
    k_refs = rest[:pages]
    v_refs = rest[pages:2 * pages]
    o_ref, acc, tail, zw = rest[2 * pages:]
    g = pl.program_id(1)
    nh, hd, page = acc.shape

    @pl.when(g == 0)
    def _():
        lane0 = lax.broadcasted_iota(jnp.int32, (1, page), 1) == 0
        pos = past_len + lax.broadcasted_iota(jnp.int32, (1, page), 1) * 0
        vis = pos < pos

        def init_head(h, carry):
            z = jnp.sum(qc_ref[h] * knc_ref[h], axis=0, keepdims=True) + bias_ref[h]
            sp, lsig = _sb_logits(z)
            acc[h] = jnp.where(vis & lane0, jnp.exp2(lsig), 0.0) * vnc_ref[h]
            tail[h] = jnp.broadcast_to(jnp.where(vis, sp, 0.0), (pages, page))
            return carry

        lax.fori_loop(0, nh, init_head, 0)

    def logits_body(h, carry):
        qh = qc_ref[h]
        rows = [jnp.sum(k_refs[p][h] * qh, axis=0, keepdims=True) for p in range(pages)]
        zw[h] = jnp.concatenate(rows, axis=0) + bias_ref[h]
        return carry

    lax.fori_loop(0, nh, logits_body, 0, unroll=2)

    r_ = lax.broadcasted_iota(jnp.int32, (page, page), 0)
    c_ = lax.broadcasted_iota(jnp.int32, (page, page), 1)
    tri = (r_ > c_).astype(BF16)
    slot = lax.broadcasted_iota(jnp.int32, (nh * pages, page), 0) % pages
    sp, lsig = _sb_logits(zw[...].reshape(nh * pages, page))
    within = _split_dot(sp, tri)
    rs = jnp.broadcast_to(jnp.sum(sp, axis=1, keepdims=True), sp.shape)
    run = rs
    sft = 1
    while sft < pages:
        run = run + jnp.where(slot >= sft, pltpu.roll(run, sft, 0), 0.0)
        sft *= 2
    tl = tail[...].reshape(nh * pages, page)
    zw[...] = jnp.exp2(lsig - within - (run - rs) - tl).reshape(nh, pages, page)
    total = run.reshape(nh, pages, page)[:, pages - 1:pages, :]
    tail[...] = tail[...] + jnp.broadcast_to(total, (nh, pages, page))

    def values_body(h, carry):
        w = zw[h]
        a = acc[h]
        for p in range(pages):
            a = a + w[p:p + 1, :] * v_refs[p][h]
        acc[h] = a
        return carry

    lax.fori_loop(0, nh, values_body, 0, unroll=2)

    @pl.when(g == pl.num_programs(1) - 1)
    def _():
        o_ref[...] = jnp.sum(acc[...], axis=2)


def sb_attention_decode(q, k_new, v_new, bias, cache_k, cache_v, page_table, pages=8):
    b, d = q.shape
    n_phys, page, nh, hd = cache_k.shape
    n_pages = page_table.shape[1]
    assert page == LANES and pages == 8 and n_pages % pages == 0
    ck = jnp.transpose(cache_k, (0, 2, 3, 1))
    cv = jnp.transpose(cache_v, (0, 2, 3, 1))
    blk = (None, nh, hd, page)
    seq = pl.BlockSpec(blk, lambda i, g, pt: (i, 0, 0, 0))

    def page_spec(p):
        return pl.BlockSpec(blk, lambda i, g, pt: (pt[i, n_pages - 1 - (g * pages + p)], 0, 0, 0))

    def cols(a):
        return jnp.broadcast_to(a.reshape(b, nh, hd, 1), (b, nh, hd, page))

    grid_spec = pltpu.PrefetchScalarGridSpec(
        num_scalar_prefetch=1,
        grid=(b, n_pages // pages),
        in_specs=[pl.BlockSpec(memory_space=pltpu.SMEM), seq, seq, seq]
        + [page_spec(p) for p in range(pages)] * 2,
        out_specs=pl.BlockSpec((None, nh, hd), lambda i, g, pt: (i, 0, 0)),
        scratch_shapes=[pltpu.VMEM((nh, hd, page), F32), pltpu.VMEM((nh, pages, page), F32),
                        pltpu.VMEM((nh, pages, page), F32)],
    )
    out = pl.pallas_call(
        functools.partial(_sb_decode_kernel, pages=pages, past_len=n_pages * page),
        grid_spec=grid_spec,
        out_shape=jax.ShapeDtypeStruct((b, nh, hd), F32),
        compiler_params=_params("arbitrary", "arbitrary"),
        name="sb_decode",
    )(page_table, bias.astype(F32), cols(q), cols(k_new), cols(v_new), *([ck] * pages), *([cv] * pages))
    return out.reshape(b, d)


def _proj_post_kernel(a_ref, w_ref, x_ref, g_ref, gate_ref, o_ref, *, paired):
    if paired:
        a = jnp.concatenate([a_ref[t] for t in range(a_ref.shape[0])], axis=1)
    else:
        a = a_ref[...]
    y = _dot(a.astype(BF16), w_ref[...])
    o_ref[...] = x_ref[...] + gate_ref[...] * _rms(y, g_ref[...])


def proj_post(a, w, x, g_all, g_idx, mod, gate_col, tm):
    m, d = x.shape
    paired = a.ndim == 4
    bpg = (m // tm) // mod.shape[0]
    if paired:
        spb = a.shape[2] // tm
        a_spec = pl.BlockSpec((None, a.shape[1], tm, PAIR), lambda i: (i // spb, 0, i % spb, 0))
    else:
        a_spec = pl.BlockSpec((tm, a.shape[1]), lambda i: (i, 0))
    return pl.pallas_call(
        functools.partial(_proj_post_kernel, paired=paired),
        grid=(m // tm,),
        in_specs=[a_spec, pl.BlockSpec(w.shape, lambda i: (0, 0)), pl.BlockSpec((tm, d), lambda i: (i, 0)),
                  _g_spec(d, g_idx), _mod_spec(mod, d, gate_col, bpg)],
        out_specs=pl.BlockSpec((tm, d), lambda i: (i, 0)),
        out_shape=jax.ShapeDtypeStruct((m, d), F32),
        compiler_params=_params("arbitrary"),
        name="proj_post",
    )(a, w, x, g_all, mod)


def _ffn_kernel(x_ref, g_ref, sh_ref, sc_ref, wg_ref, wu_ref, wd_ref, gp_ref, gate_ref, o_ref, h_scr, acc):
    f = pl.program_id(1)

    @pl.when(f == 0)
    def _():
        h_scr[...] = _normmod(x_ref[...], g_ref[...], sh_ref[...], sc_ref[...]).astype(BF16)
        acc[...] = jnp.zeros_like(acc)

    h = h_scr[...]
    act = _silu(_dot(h, wg_ref[...])) * _dot(h, wu_ref[...])
    acc[...] += _dot(act.astype(BF16), wd_ref[...])

    @pl.when(f == pl.num_programs(1) - 1)
    def _():
        o_ref[...] = x_ref[...] + gate_ref[...] * _rms(acc[...], gp_ref[...])


def ffn_sublayer(x, g_all, g_pre, g_post, mod, sh_col, sc_col, gate_col, w_gu, w_dn, tm, fc):
    m, d = x.shape
    dff = w_dn.shape[0]
    nf = dff // fc
    bpg = (m // tm) // mod.shape[0]
    wmode = dict(pipeline_mode=pl.Buffered(1)) if nf == 1 else {}
    return pl.pallas_call(
        _ffn_kernel,
        grid=(m // tm, nf),
        in_specs=[pl.BlockSpec((tm, d), lambda i, f: (i, 0)),
                  _g_spec(d, g_pre), _mod_spec(mod, d, sh_col, bpg), _mod_spec(mod, d, sc_col, bpg),
                  pl.BlockSpec((d, fc), lambda i, f: (0, f), **wmode),
                  pl.BlockSpec((d, fc), lambda i, f: (0, nf + f), **wmode),
                  pl.BlockSpec((fc, d), lambda i, f: (f, 0), **wmode),
                  _g_spec(d, g_post), _mod_spec(mod, d, gate_col, bpg)],
        out_specs=pl.BlockSpec((tm, d), lambda i, f: (i, 0)),
        out_shape=jax.ShapeDtypeStruct((m, d), F32),
        scratch_shapes=[pltpu.VMEM((tm, d), BF16), pltpu.VMEM((tm, d), F32)],
        compiler_params=_params("arbitrary", "arbitrary"),
        name="ffn",
    )(x, g_all, mod, mod, w_gu, w_gu, w_dn, g_all, mod)


def _expand_heads(a, e_bf16):
    return _split_dot(a, e_bf16)


def _ssd_prompt_kernel(z_ref, x_ref, bc_ref, dt_ref, cwx_ref, cwbc_ref, cbx_ref, cbbc_ref, dtb_ref, a_ref,
                       dsk_ref, ng_ref, e_ref, y_ref, st_ref, conv_ref, xs_x, xs_bc, y_scr,
                       *, chunk, groups, dstate):
    c = pl.program_id(1)
    last = c == pl.num_programs(1) - 1
    L = chunk
    d_inner = x_ref.shape[1]
    npairs = d_inner // PAIR
    ppg = npairs // groups
    kw = cwx_ref.shape[0]

    @pl.when(c == 0)
    def _():
        xs_x[0:8, :] = jnp.zeros((8, xs_x.shape[1]), F32)
        xs_bc[0:8, :] = jnp.zeros((8, xs_bc.shape[1]), F32)
        st_ref[...] = jnp.zeros_like(st_ref)

    def conv_act(cur_ref, xs, cw_ref, cb_ref):
        xs[8:8 + L, :] = cur_ref[...]
        out = cb_ref[...] + cw_ref[kw - 1:kw, :] * cur_ref[...]
        for k in range(kw - 1):
            sft = kw - 1 - k
            out = out + cw_ref[k:k + 1, :] * xs[8 - sft:8 - sft + L, :]
        return _silu(out)

    x_act = conv_act(x_ref, xs_x, cwx_ref, cbx_ref)
    bc_act = conv_act(bc_ref, xs_bc, cwbc_ref, cbbc_ref)

    @pl.when(last)
    def _():
        conv_ref[:, 0:d_inner] = xs_x[8 + L - (kw - 1):8 + L, :]
        conv_ref[:, d_inner:] = xs_bc[8 + L - (kw - 1):8 + L, :]

    xs_x[0:8, :] = xs_x[L:L + 8, :]
    xs_bc[0:8, :] = xs_bc[L:L + 8, :]

    dt = _softplus(dt_ref[...] + dtb_ref[...])
    dta = dt * a_ref[...]
    r_ = lax.broadcasted_iota(jnp.int32, (L, L), 0)
    c_ = lax.broadcasted_iota(jnp.int32, (L, L), 1)
    causal = r_ >= c_
    a_cs = _dot_hi(causal.astype(F32), dta)
    a_cs_t = a_cs.T
    a_end = a_cs[L - 1:L, :]
    e = e_ref[...]
    dt_x = _expand_heads(dt, e)
    grow_x = _expand_heads(jnp.exp(a_cs), e)
    dend_x = _expand_heads(jnp.exp(a_end - a_cs), e)
    cdec = jnp.exp(a_end)

    low = lax.broadcasted_iota(jnp.int32, (1, PAIR), 1) < HEAD_DIM
    rlow = lax.broadcasted_iota(jnp.int32, (PAIR, 1), 0) < HEAD_DIM
    gw = groups * dstate
    for g in range(groups):
        bg = bc_act[:, g * dstate:(g + 1) * dstate].astype(BF16)
        cg = bc_act[:, gw + g * dstate:gw + (g + 1) * dstate].astype(BF16)
        cb = lax.dot_general(cg, bg, NT, preferred_element_type=F32)
        for pp in range(ppg):
            hp = g * ppg + pp
            sl = slice(hp * PAIR, (hp + 1) * PAIR)
            xdt = x_act[:, sl] * dt_x[:, sl]
            xdt_b = xdt.astype(BF16)
            y = jnp.zeros((L, PAIR), F32)
            for hh in range(2):
                h = 2 * hp + hh
                seg = a_cs[:, h:h + 1] - a_cs_t[h:h + 1, :]
                dec = jnp.where(causal, jnp.exp(seg), 0.0)
                sel = low if hh == 0 else jnp.logical_not(low)
                y = y + _dot((cb * dec).astype(BF16), jnp.where(sel, xdt_b, jnp.zeros_like(xdt_b)))
            prev = st_ref[hp]
            y = y + lax.dot_general(cg, prev.astype(BF16), NT, preferred_element_type=F32) * grow_x[:, sl]
            y_scr[:, sl] = y
            xw = (xdt * dend_x[:, sl]).astype(BF16)
            s_c = lax.dot_general(xw, bg, TN, preferred_element_type=F32)
            cd = jnp.where(rlow, cdec[:, 2 * hp:2 * hp + 1], cdec[:, 2 * hp + 1:2 * hp + 2])
            st_ref[hp] = prev * cd + s_c

    y = y_scr[...] + x_act * dsk_ref[...]
    y = y * _silu(z_ref[...])
    gsz = d_inner // groups
    for g in range(groups):
        sl = slice(g * gsz, (g + 1) * gsz)
        y_ref[:, sl] = _rms(y[:, sl], ng_ref[:, sl]).astype(BF16)


def _head_expander(nheads):
    lane = jnp.arange(nheads * HEAD_DIM)[None, :] // HEAD_DIM
    return (lane == jnp.arange(LANES)[:, None]).astype(BF16)


def ssd_prompt(zx, dt_raw, conv_w, conv_b, dt_bias, a_neg, d_skip, norm_g, batch, chunk, groups, dstate):
    m = zx.shape[0]
    d_inner = d_skip.shape[0] * HEAD_DIM
    nheads = d_skip.shape[0]
    bcw = 2 * groups * dstate
    s = m // batch
    nc = s // chunk
    kw = conv_w.shape[0]
    npairs = d_inner // PAIR
    pad = LANES - nheads
    dtb = jnp.pad(dt_bias.astype(F32), (0, pad)).reshape(1, LANES)
    a_p = jnp.pad(a_neg, (0, pad)).reshape(1, LANES)
    dsk = jnp.repeat(d_skip.astype(F32), HEAD_DIM).reshape(1, d_inner)
    cb2 = conv_b.reshape(1, -1)
    rb = lambda b, c: b * nc + c
    const = lambda shape, j=0: pl.BlockSpec(shape, lambda b, c: (0, j))
    return pl.pallas_call(
        functools.partial(_ssd_prompt_kernel, chunk=chunk, groups=groups, dstate=dstate),
        grid=(batch, nc),
        in_specs=[pl.BlockSpec((chunk, d_inner), lambda b, c: (rb(b, c), 0)),
                  pl.BlockSpec((chunk, d_inner), lambda b, c: (rb(b, c), 1)),
                  pl.BlockSpec((chunk, bcw), lambda b, c: (rb(b, c), 2 * d_inner // bcw)),
                  pl.BlockSpec((chunk, LANES), lambda b, c: (rb(b, c), 0)),
                  const((kw, d_inner)), const((kw, bcw), d_inner // bcw),
                  const((1, d_inner)), const((1, bcw), d_inner // bcw),
                  const((1, LANES)), const((1, LANES)), const((1, d_inner)), const((1, d_inner)),
                  const((LANES, d_inner))],
        out_specs=[pl.BlockSpec((chunk, d_inner), lambda b, c: (rb(b, c), 0)),
                   pl.BlockSpec((None, npairs, PAIR, dstate), lambda b, c: (b, 0, 0, 0)),
                   pl.BlockSpec((None, kw - 1, d_inner + bcw), lambda b, c: (b, 0, 0))],
        out_shape=[jax.ShapeDtypeStruct((m, d_inner), BF16),
                   jax.ShapeDtypeStruct((batch, npairs, PAIR, dstate), F32),
                   jax.ShapeDtypeStruct((batch, kw - 1, d_inner + bcw), F32)],
        scratch_shapes=[pltpu.VMEM((chunk + 8, d_inner), F32), pltpu.VMEM((chunk + 8, bcw), F32),
                        pltpu.VMEM((chunk, d_inner), F32)],
        compiler_params=_params("arbitrary", "arbitrary"),
        name="ssd_prompt",
    )(zx, zx, zx, dt_raw, conv_w, conv_w, cb2, cb2, dtb, a_p, dsk, norm_g.reshape(1, d_inner),
      _head_expander(nheads))


def _ssd_step_kernel(z_ref, x_ref, bc_ref, dt_ref, sx_ref, sbc_ref, st_ref, cwx_ref, cwbc_ref, cbx_ref, cbbc_ref,
                     dtb_ref, a_ref, dsk_ref, ng_ref, y_ref, sto_ref, conv_ref, y_scr, *, groups, dstate):
    d_inner = x_ref.shape[1]
    npairs = d_inner // PAIR
    ppg = npairs // groups
    kw = cwx_ref.shape[0]

    def conv_act(new_ref, s_ref, cw_ref, cb_ref):
        out = cb_ref[...] + cw_ref[kw - 1:kw, :] * new_ref[...]
        for k in range(kw - 1):
            out = out + cw_ref[k:k + 1, :] * s_ref[k:k + 1, :]
        return _silu(out)

    x_act = conv_act(x_ref, sx_ref, cwx_ref, cbx_ref)
    bc_act = conv_act(bc_ref, sbc_ref, cwbc_ref, cbbc_ref)
    conv_ref[0:kw - 2, 0:d_inner] = sx_ref[1:kw - 1, :]
    conv_ref[0:kw - 2, d_inner:] = sbc_ref[1:kw - 1, :]
    conv_ref[kw - 2:kw - 1, 0:d_inner] = x_ref[...]
    conv_ref[kw - 2:kw - 1, d_inner:] = bc_ref[...]

    dt = _softplus(dt_ref[...] + dtb_ref[...])
    decay = jnp.exp(dt * a_ref[...])
    rlow = lax.broadcasted_iota(jnp.int32, (PAIR, 1), 0) < HEAD_DIM
    low = lax.broadcasted_iota(jnp.int32, (1, PAIR), 1) < HEAD_DIM
    eye = lax.broadcasted_iota(jnp.int32, (PAIR, PAIR), 0) == lax.broadcasted_iota(jnp.int32, (PAIR, PAIR), 1)
    gw = groups * dstate
    for hp in range(npairs):
        g = hp // ppg
        sl = slice(hp * PAIR, (hp + 1) * PAIR)
        bg = jnp.broadcast_to(bc_act[:, g * dstate:(g + 1) * dstate], (PAIR, dstate))
        cg = jnp.broadcast_to(bc_act[:, gw + g * dstate:gw + (g + 1) * dstate], (8, dstate))
        dt_p = jnp.where(low, dt[:, 2 * hp:2 * hp + 1], dt[:, 2 * hp + 1:2 * hp + 2])
        xdt = x_act[:, sl] * dt_p
        outer = _dot_hi(jnp.where(eye, jnp.broadcast_to(xdt, (PAIR, PAIR)), 0.0), bg)
        dc = jnp.where(rlow, decay[:, 2 * hp:2 * hp + 1], decay[:, 2 * hp + 1:2 * hp + 2])
        new = st_ref[hp] * dc + outer
        sto_ref[hp] = new
        y_scr[:, sl] = _dot_hi(cg, new, NT)[0:1, :]

    y = y_scr[...] + x_act * dsk_ref[...]
    y = y * _silu(z_ref[...])
    gsz = d_inner // groups
    for g in range(groups):
        sl = slice(g * gsz, (g + 1) * gsz)
        y_ref[:, sl] = _rms(y[:, sl], ng_ref[:, sl]).astype(BF16)


def ssd_step(zx, dt_raw, state_conv, state_ssm, conv_w, conv_b, dt_bias, a_neg, d_skip, norm_g, groups, dstate):
    b = zx.shape[0]
    nheads = d_skip.shape[0]
    d_inner = nheads * HEAD_DIM
    bcw = 2 * groups * dstate
    kw = conv_w.shape[0]
    npairs = d_inner // PAIR
    pad = LANES - nheads
    dtb = jnp.pad(dt_bias.astype(F32), (0, pad)).reshape(1, LANES)
    a_p = jnp.pad(a_neg, (0, pad)).reshape(1, LANES)
    dsk = jnp.repeat(d_skip.astype(F32), HEAD_DIM).reshape(1, d_inner)
    cb2 = conv_b.reshape(1, -1)
    zx3 = zx.reshape(b, 1, -1)
    st = state_ssm.reshape(b, npairs, PAIR, dstate)
    const = lambda shape, j=0: pl.BlockSpec(shape, lambda i: (0, j))
    y, st_new, conv_new = pl.pallas_call(
        functools.partial(_ssd_step_kernel, groups=groups, dstate=dstate),
        grid=(b,),
        in_specs=[pl.BlockSpec((None, 1, d_inner), lambda i: (i, 0, 0)),
                  pl.BlockSpec((None, 1, d_inner), lambda i: (i, 0, 1)),
                  pl.BlockSpec((None, 1, bcw), lambda i: (i, 0, 2 * d_inner // bcw)),
                  pl.BlockSpec((None, 1, LANES), lambda i: (i, 0, 0)),
                  pl.BlockSpec((None, kw - 1, d_inner), lambda i: (i, 0, 0)),
                  pl.BlockSpec((None, kw - 1, bcw), lambda i: (i, 0, d_inner // bcw)),
                  pl.BlockSpec((None, npairs, PAIR, dstate), lambda i: (i, 0, 0, 0)),
                  const((kw, d_inner)), const((kw, bcw), d_inner // bcw),
                  const((1, d_inner)), const((1, bcw), d_inner // bcw),
                  const((1, LANES)), const((1, LANES)), const((1, d_inner)), const((1, d_inner))],
        out_specs=[pl.BlockSpec((None, 1, d_inner), lambda i: (i, 0, 0)),
                   pl.BlockSpec((None, npairs, PAIR, dstate), lambda i: (i, 0, 0, 0)),
                   pl.BlockSpec((None, kw - 1, d_inner + bcw), lambda i: (i, 0, 0))],
        out_shape=[jax.ShapeDtypeStruct((b, 1, d_inner), BF16),
                   jax.ShapeDtypeStruct((b, npairs, PAIR, dstate), F32),
                   jax.ShapeDtypeStruct((b, kw - 1, d_inner + bcw), F32)],
        scratch_shapes=[pltpu.VMEM((1, d_inner), F32)],
        compiler_params=_params("arbitrary"),
        name="ssd_step",
    )(zx3, zx3, zx3, dt_raw.reshape(b, 1, LANES), state_conv, state_conv, st, conv_w, conv_w, cb2, cb2,
      dtb, a_p, dsk, norm_g.reshape(1, d_inner))
    return y.reshape(b, d_inner), st_new, conv_new


def _store_token_rows(ref, val):
    n, d = val.shape
    r = d // LANES
    for c in range(r):
        ref[pl.ds(c, n, stride=r), :] = val[:, c * LANES:(c + 1) * LANES]


def _load_token_rows(ref, n, lead=()):
    r = ref.shape[-2] // n
    return jnp.concatenate([ref[(*lead, pl.ds(c, n, stride=r), slice(None))] for c in range(r)], axis=1)


def _token_rows(ref, t, r):
    return ref.at[pl.ds(pl.multiple_of(t * r, r), r)]


def _router_kernel(x_ref, g_ref, sh_ref, sc_ref, wr_ref, h_ref, route_ref, *, n_experts):
    h = _normmod(x_ref[...], g_ref[...], sh_ref[...], sc_ref[...])
    _store_token_rows(h_ref, h)
    logits = _dot_hi(h, wr_ref[...])
    lane = lax.broadcasted_iota(jnp.int32, logits.shape, 1).astype(F32)
    neg = jnp.float32(-jnp.inf)
    lg = jnp.where(lane < n_experts, logits, neg)
    m1 = jnp.max(lg, axis=1, keepdims=True)
    i1 = jnp.min(jnp.where(lg == m1, lane, float(LANES)), axis=1, keepdims=True)
    lg2 = jnp.where(lane == i1, neg, lg)
    m2 = jnp.max(lg2, axis=1, keepdims=True)
    i2 = jnp.min(jnp.where(lg2 == m2, lane, float(LANES)), axis=1, keepdims=True)
    e2 = jnp.exp(m2 - m1)
    g1 = 1.0 / (1.0 + e2)
    route_ref[...] = (jnp.where(lane == 0.0, i1, 0.0) + jnp.where(lane == 1.0, i2, 0.0)
                      + jnp.where(lane == 2.0, g1, 0.0) + jnp.where(lane == 3.0, e2 * g1, 0.0))


def moe_router(x, g_all, g_idx, mod, sh_col, sc_col, w_router, tm):
    m, d = x.shape
    n_experts = w_router.shape[1]
    bpg = (m // tm) // mod.shape[0]
    wr = jnp.pad(w_router.astype(F32), ((0, 0), (0, LANES - n_experts)))
    return pl.pallas_call(
        functools.partial(_router_kernel, n_experts=n_experts),
        grid=(m // tm,),
        in_specs=[pl.BlockSpec((tm, d), lambda i: (i, 0)),
                  _g_spec(d, g_idx), _mod_spec(mod, d, sh_col, bpg), _mod_spec(mod, d, sc_col, bpg),
                  pl.BlockSpec((d, LANES), lambda i: (0, 0))],
        out_specs=[pl.BlockSpec((tm * d // LANES, LANES), lambda i: (i, 0)),
                   pl.BlockSpec((tm, LANES), lambda i: (i, 0))],
        out_shape=[jax.ShapeDtypeStruct((m * d // LANES, LANES), F32), jax.ShapeDtypeStruct((m, LANES), F32)],
        compiler_params=_params("arbitrary"),
        name="moe_router",
    )(x, g_all, mod, mod, wr)


def moe_plan(routes, n_experts, tm):
    t = routes.shape[0]
    e = routes[:, :2].astype(jnp.int32).reshape(-1)
    onehot = (e[:, None] == jnp.arange(n_experts, dtype=jnp.int32)[None, :]).astype(jnp.int32)
    before = jnp.cumsum(onehot, axis=0) - onehot
    counts = jnp.sum(onehot, axis=0)
    padded = (counts + tm - 1) // tm * tm
    ends = jnp.cumsum(padded)
    pos = jnp.sum(onehot * (before + (ends - padded)[None, :]), axis=1)
    nb = -(-(2 * t + n_experts * (tm - 1)) // tm)
    blk_start = jnp.arange(nb, dtype=jnp.int32) * tm
    blk_e = jnp.minimum(jnp.sum((blk_start[:, None] >= ends[None, :]).astype(jnp.int32), axis=1), n_experts - 1)
    live = (blk_start < ends[-1]).astype(jnp.int32)
    blk_e = jnp.where(live == 1, blk_e, blk_e[jnp.maximum(jnp.sum(live) - 1, 0)])
    return pos.astype(jnp.int32), blk_e.astype(jnp.int32), live


def _slot_spec(tb, copies):
    return pl.BlockSpec((None, 1, copies * tb), lambda i, *_: (i, 0, 0), memory_space=pltpu.SMEM)


def _dispatch_kernel(pos_ref, h_ref, buf_in_ref, buf_ref, sem, *, copies, r):
    del buf_in_ref
    n = h_ref.shape[0] // r

    def row_copy(t, c):
        return pltpu.make_async_copy(_token_rows(h_ref, t, r), _token_rows(buf_ref, pos_ref[0, copies * t + c], r), sem)

    def start(t, carry):
        for c in range(copies):
            row_copy(t, c).start(priority=c % 2)
        return carry

    def wait(t, carry):
        for c in range(copies):
            row_copy(t, c).wait()
        return carry

    lax.fori_loop(0, n, start, 0, unroll=4)
    lax.fori_loop(0, n, wait, 0, unroll=4)


def moe_dispatch(h, pos, buf, tb, r, copies=2):
    m = h.shape[0] // r
    return pl.pallas_call(
        functools.partial(_dispatch_kernel, copies=copies, r=r),
        grid=(m // tb,),
        in_specs=[_slot_spec(tb, copies), pl.BlockSpec((tb * r, LANES), lambda i: (i, 0)),
                  pl.BlockSpec(memory_space=pl.ANY)],
        out_specs=pl.BlockSpec(memory_space=pl.ANY),
        out_shape=jax.ShapeDtypeStruct(buf.shape, buf.dtype),
        scratch_shapes=[pltpu.SemaphoreType.DMA(())],
        input_output_aliases={2: 0},
        compiler_params=_params("arbitrary"),
        name="moe_dispatch",
    )(pos.reshape(m // tb, 1, copies * tb), h, buf)


def _moe_experts_kernel(be_ref, live_ref, h_ref, wg_ref, wu_ref, wd_ref, y_ref, hb, acc):
    f = pl.program_id(1)
    live = live_ref[pl.program_id(0)] == 1

    @pl.when(live & (f == 0))
    def _():
        hb[...] = _load_token_rows(h_ref, hb.shape[0]).astype(BF16)
        acc[...] = jnp.zeros_like(acc)

    @pl.when(live)
    def _():
        h = hb[...]
        act = _silu(_dot(h, wg_ref[...])) * _dot(h, wu_ref[...])
        acc[...] += _dot(act.astype(BF16), wd_ref[...])

    @pl.when(f == pl.num_programs(1) - 1)
    def _():
        _store_token_rows(y_ref, jnp.where(live, acc[...], 0.0))


def moe_experts(h_sorted, blk_e, live, w_gu, w_dn, tm, fc):
    d, dff = w_dn.shape[2], w_dn.shape[1]
    r = d // LANES
    p = h_sorted.shape[0] // r
    nf = dff // fc

    def fe(b, f, live_):
        return jnp.where(live_[b] == 1, f, nf - 1)

    grid_spec = pltpu.PrefetchScalarGridSpec(
        num_scalar_prefetch=2,
        grid=(p // tm, nf),
        in_specs=[pl.BlockSpec((tm * r, LANES), lambda b, f, be, lv: (b, 0)),
                  pl.BlockSpec((None, d, fc), lambda b, f, be, lv: (be[b], 0, fe(b, f, lv))),
                  pl.BlockSpec((None, d, fc), lambda b, f, be, lv: (be[b], 0, nf + fe(b, f, lv))),
                  pl.BlockSpec((None, fc, d), lambda b, f, be, lv: (be[b], fe(b, f, lv), 0))],
        out_specs=pl.BlockSpec((tm * r, LANES), lambda b, f, be, lv: (b, 0)),
        scratch_shapes=[pltpu.VMEM((tm, d), BF16), pltpu.VMEM((tm, d), F32)],
    )
    return pl.pallas_call(
        _moe_experts_kernel,
        grid_spec=grid_spec,
        out_shape=jax.ShapeDtypeStruct((p * r, LANES), F32),
        compiler_params=_params("arbitrary", "arbitrary"),
        name="moe_experts",
    )(blk_e, live, h_sorted, w_gu, w_gu, w_dn)


def _moe_combine_kernel(pos_ref, y_hbm, route_ref, x_ref, gp_ref, gate_ref, o_ref, ybuf, sem):
    tb, d = x_ref.shape
    r = d // LANES

    def row_copy(t, c):
        return pltpu.make_async_copy(_token_rows(y_hbm, pos_ref[0, 2 * t + c], r), _token_rows(ybuf.at[c], t, r), sem)

    def start(t, carry):
        for c in range(2):
            row_copy(t, c).start(priority=c % 2)
        return carry

    def wait(t, carry):
        for c in range(2):
            row_copy(t, c).wait()
        return carry

    lax.fori_loop(0, tb, start, 0, unroll=4)
    lax.fori_loop(0, tb, wait, 0, unroll=4)
    rec = route_ref[...]
    lane = lax.broadcasted_iota(jnp.int32, rec.shape, 1)
    g1 = jnp.sum(jnp.where(lane == 2, rec, 0.0), axis=1, keepdims=True)
    g2 = jnp.sum(jnp.where(lane == 3, rec, 0.0), axis=1, keepdims=True)
    y = g1 * _load_token_rows(ybuf, tb, (0,)) + g2 * _load_token_rows(ybuf, tb, (1,))
    o_ref[...] = x_ref[...] + gate_ref[...] * _rms(y, gp_ref[...])


def moe_combine(y_sorted, pos, routes, x, g_all, g_post, mod, gate_col, tb):
    m, d = x.shape
    bpg = (m // tb) // mod.shape[0]
    return pl.pallas_call(
        _moe_combine_kernel,
        grid=(m // tb,),
        in_specs=[_slot_spec(tb, 2), pl.BlockSpec(memory_space=pl.ANY),
                  pl.BlockSpec((tb, LANES), lambda i: (i, 0)), pl.BlockSpec((tb, d), lambda i: (i, 0)),
                  _g_spec(d, g_post), _mod_spec(mod, d, gate_col, bpg)],
        out_specs=pl.BlockSpec((tb, d), lambda i: (i, 0)),
        out_shape=jax.ShapeDtypeStruct((m, d), F32),
        scratch_shapes=[pltpu.VMEM((2, tb * d // LANES, LANES), F32), pltpu.SemaphoreType.DMA(())],
        compiler_params=_params("arbitrary"),
        name="moe_combine",
    )(pos.reshape(m // tb, 1, 2 * tb), y_sorted, routes, x, g_all, mod)


def kernel(x_prompt, x_sample, cache_k, cache_v, state_ssm, state_conv, page_table, c_prompt, c_sample, norm_g, w_ada, b_ada, sb_w_qkv, sb_w_o, sb_logit_bias, ssd_w_in, ssd_conv_w, ssd_conv_b, ssd_dt_bias, ssd_a_log, ssd_d, ssd_norm_g, ssd_w_out, ffn_w_gu, ffn_w_dn, moe_w_router, moe_w_gu, moe_w_dn):
    bp, seq, d = x_prompt.shape
    bs = x_sample.shape[0]
    nh_sb = sb_logit_bias.shape[0]
    nh_ssd = ssd_d.shape[0]
    d_inner = nh_ssd * HEAD_DIM
    dstate = state_ssm.shape[-1]
    groups = (ssd_conv_w.shape[1] - d_inner) // (2 * dstate)
    n_zx = d_inner + ssd_conv_w.shape[1]
    mp = bp * seq
    tm = min(512, seq)
    tm_big = min(1024, seq)

    c_all = jnp.concatenate([c_prompt, c_sample], axis=0)
    c_all = jnp.pad(c_all, ((0, -c_all.shape[0] % 8), (0, 0)))
    ada = adaln_all(c_all, w_ada, b_ada)
    g_all = norm_g.reshape(-1, 1, d)

    log2e = 1.4426950408889634
    qscale = jnp.concatenate([jnp.full((d,), log2e * HEAD_DIM ** -0.5, F32), jnp.ones((2 * d,), F32)])
    sb_bias = sb_logit_bias.astype(F32) * log2e
    w_qkv = (sb_w_qkv * qscale).astype(BF16)
    w_o = sb_w_o.astype(BF16)
    w_gu = ffn_w_gu.astype(BF16)
    w_dn = ffn_w_dn.astype(BF16)
    w_in = ssd_w_in[:, :n_zx].astype(BF16)
    w_dt = jnp.pad(ssd_w_in[:, n_zx:], ((0, 0), (0, LANES - nh_ssd))).astype(BF16)
    w_out = ssd_w_out.astype(BF16)
    mw_gu = moe_w_gu.astype(BF16)
    mw_dn = moe_w_dn.astype(BF16)
    a_neg = -jnp.exp(ssd_a_log.astype(F32))

    xp = x_prompt.reshape(mp, d)
    xs = x_sample.reshape(bs, d)

    mod_p = ada[0, :bp].reshape(bp, 1, 6 * d)
    mod_s = ada[0, bp:bp + bs].reshape(1, bs, 6 * d)
    q_pairs, k_t, v_t, kv_t = qkv_prompt(xp, g_all, 0, mod_p, 0, 1, w_qkv[:, :d], w_qkv[:, d:].T, bp, tm)
    o_p = sb_attention_prompt(q_pairs, kv_t, sb_bias, t=min(256, seq))
    xp = proj_post(o_p, w_o, xp, g_all, 1, mod_p, 2, tm)
    xp = ffn_sublayer(xp, g_all, 2, 3, mod_p, 3, 4, 5, w_gu, w_dn, tm, w_dn.shape[0])

    qkv_s = nm_matmul(xs, g_all, 0, mod_s, 0, 1, w_qkv, bs, 512)
    q_s, k_s, v_s = qkv_s[:, :d], qkv_s[:, d:2 * d], qkv_s[:, 2 * d:]
    o_s = sb_attention_decode(q_s, k_s, v_s, sb_bias, cache_k, cache_v, page_table)
    xs = proj_post(o_s, w_o, xs, g_all, 1, mod_s, 2, bs)
    xs = ffn_sublayer(xs, g_all, 2, 3, mod_s, 3, 4, 5, w_gu, w_dn, bs, w_dn.shape[0] // 2)

    mod_p = ada[1, :bp].reshape(bp, 1, 6 * d)
    mod_s = ada[1, bp:bp + bs].reshape(1, bs, 6 * d)
    zx_p, dt_p = nm_matmul(xp, g_all, 4, mod_p, 0, 1, w_in, tm, n_zx, extra_w=(w_dt,))
    y_p, ssm_p, conv_p = ssd_prompt(zx_p, dt_p, ssd_conv_w, ssd_conv_b, ssd_dt_bias, a_neg, ssd_d, ssd_norm_g,
                                    bp, min(128, seq), groups, dstate)
    xp = proj_post(y_p, w_out, xp, g_all, 5, mod_p, 2, tm)

    zx_s, dt_s = nm_matmul(xs, g_all, 4, mod_s, 0, 1, w_in, bs, 512, extra_w=(w_dt,))
    y_s, ssm_s, conv_s = ssd_step(zx_s, dt_s, state_conv, state_ssm, ssd_conv_w, ssd_conv_b, ssd_dt_bias, a_neg,
                                  ssd_d, ssd_norm_g, groups, dstate)
    xs = proj_post(y_s, w_out, xs, g_all, 5, mod_s, 2, bs)

    n_experts = moe_w_router.shape[1]
    tb = min(256, seq)
    h_p, route_p = moe_router(xp, g_all, 6, mod_p, 3, 4, moe_w_router, tm)
    h_s, route_s = moe_router(xs, g_all, 6, mod_s, 3, 4, moe_w_router, bs)
    pos, blk_e, live = moe_plan(jnp.concatenate([route_p, route_s], axis=0), n_experts, tm)
    rpt = d // LANES
    h_sorted = jnp.zeros((blk_e.shape[0] * tm * rpt, LANES), F32)
    h_sorted = moe_dispatch(h_p, pos[:2 * mp], h_sorted, tb, rpt)
    h_sorted = moe_dispatch(h_s, pos[2 * mp:], h_sorted, bs, rpt)
    y_sorted = moe_experts(h_sorted, blk_e, live, mw_gu, mw_dn, tm, mw_dn.shape[1] // 2)
    xp = moe_combine(y_sorted, pos[:2 * mp], route_p, xp, g_all, 7, mod_p, 5, tb)
    xs = moe_combine(y_sorted, pos[2 * mp:], route_s, xs, g_all, 7, mod_s, 5, bs)

    hd = (nh_sb, d // nh_sb)
    return (xp.reshape(bp, seq, d), xs.reshape(bs, 1, d),
            jnp.transpose(k_t, (0, 3, 1, 2)), jnp.transpose(v_t, (0, 3, 1, 2)),
            k_s.reshape(bs, 1, *hd), v_s.reshape(bs, 1, *hd),
            ssm_p.reshape(bp, nh_ssd, HEAD_DIM, dstate), conv_p,
            ssm_s.reshape(bs, nh_ssd, HEAD_DIM, dstate), conv_s)
```

```python
import functools

import jax
import jax.numpy as jnp
from jax import lax
from jax.experimental import pallas as pl
from jax.experimental.pallas import tpu as pltpu

F32 = jnp.float32
BF16 = jnp.bfloat16
EPS = 1e-6
LANES = 128
HEAD_DIM = 64
PAIR = 2 * HEAD_DIM
VMEM_LIMIT = 56 * 1024 * 1024

NT = (((1,), (1,)), ((), ()))
TN = (((0,), (0,)), ((), ()))


def _params(*sem):
    return pltpu.CompilerParams(dimension_semantics=sem, vmem_limit_bytes=VMEM_LIMIT)


def _silu(x):
    return (0.5 * x) * (1.0 + jnp.tanh(0.5 * x))


def _softplus(z):
    return jnp.maximum(z, 0.0) + jnp.log(1.0 + jnp.exp(-jnp.abs(z)))


def _sb_logits(z):
    sp = jnp.maximum(z, 0.0) + jnp.log2(1.0 + jnp.exp2(jnp.minimum(z, -z)))
    return sp, z - sp


def _rms(x, g):
    return x * lax.rsqrt(jnp.mean(x * x, axis=-1, keepdims=True) + EPS) * g


def _normmod(x, g, shift, scale):
    return _rms(x, g) * (1.0 + scale) + shift


def _dot(a, b):
    return jnp.dot(a, b, preferred_element_type=F32)


def _dot_hi(a, b, dims=None):
    if dims is None:
        return jnp.dot(a, b, preferred_element_type=F32, precision=lax.Precision.HIGHEST)
    return lax.dot_general(a, b, dims, preferred_element_type=F32, precision=lax.Precision.HIGHEST)


def _split_dot(a, b_bf16):
    hi = a.astype(BF16)
    lo = (a - hi.astype(F32)).astype(BF16)
    return _dot(hi, b_bf16) + _dot(lo, b_bf16)


def _mod_spec(mod, d, col, bpg):
    r = mod.shape[1]
    return pl.BlockSpec((None, r, d), lambda i, *_: (i // bpg, 0, col))


def _g_spec(d, idx):
    return pl.BlockSpec((None, 1, d), lambda i, *_: (idx, 0, 0))


def _adaln_kernel(c_ref, w_ref, b_ref, o_ref):
    s = _silu(c_ref[...]).astype(BF16)
    o_ref[...] = _dot(s, w_ref[...].astype(BF16)) + b_ref[...]


def adaln_all(c, w_ada, b_ada, tn=1024):
    depth, d, n = w_ada.shape
    m = c.shape[0]
    return pl.pallas_call(
        _adaln_kernel,
        grid=(depth, n // tn),
        in_specs=[pl.BlockSpec((m, d), lambda l, j: (0, 0)),
                  pl.BlockSpec((None, d, tn), lambda l, j: (l, 0, j)),
                  pl.BlockSpec((None, 1, tn), lambda l, j: (l, 0, j))],
        out_specs=pl.BlockSpec((None, m, tn), lambda l, j: (l, 0, j)),
        out_shape=jax.ShapeDtypeStruct((depth, m, n), F32),
        compiler_params=_params("arbitrary", "arbitrary"),
        name="adaln",
    )(c, w_ada, b_ada.reshape(depth, 1, n))


def _nm_matmul_kernel(x_ref, g_ref, sh_ref, sc_ref, w_ref, *rest, n_extra):
    extra_w = rest[:n_extra]
    o_ref = rest[n_extra]
    extra_o = rest[n_extra + 1:2 * n_extra + 1]
    h_scr = rest[-1]

    @pl.when(pl.program_id(1) == 0)
    def _():
        h_scr[...] = _normmod(x_ref[...], g_ref[...], sh_ref[...], sc_ref[...]).astype(BF16)
        for w2, o2 in zip(extra_w, extra_o):
            o2[...] = _dot(h_scr[...], w2[...])

    o_ref[...] = _dot(h_scr[...], w_ref[...])


def nm_matmul(x, g_all, g_idx, mod, sh_col, sc_col, w, tm, tn, extra_w=()):
    m, d = x.shape
    n = w.shape[1]
    bpg = (m // tm) // mod.shape[0]
    in_specs = [pl.BlockSpec((tm, d), lambda i, j: (i, 0)),
                _g_spec(d, g_idx), _mod_spec(mod, d, sh_col, bpg), _mod_spec(mod, d, sc_col, bpg),
                pl.BlockSpec((d, tn), lambda i, j: (0, j),
                             **(dict(pipeline_mode=pl.Buffered(1)) if tn == n else {}))]
    out_specs = [pl.BlockSpec((tm, tn), lambda i, j: (i, j))]
    out_shape = [jax.ShapeDtypeStruct((m, n), F32)]
    for w2 in extra_w:
        in_specs.append(pl.BlockSpec(w2.shape, lambda i, j: (0, 0)))
        out_specs.append(pl.BlockSpec((tm, w2.shape[1]), lambda i, j: (i, 0)))
        out_shape.append(jax.ShapeDtypeStruct((m, w2.shape[1]), F32))
    outs = pl.pallas_call(
        functools.partial(_nm_matmul_kernel, n_extra=len(extra_w)),
        grid=(m // tm, n // tn),
        in_specs=in_specs, out_specs=out_specs, out_shape=out_shape,
        scratch_shapes=[pltpu.VMEM((tm, d), BF16)],
        compiler_params=_params("arbitrary", "arbitrary"),
        name="nm_matmul",
    )(x, g_all, mod, mod, w, *extra_w)
    return outs if extra_w else outs[0]


def _qkv_kernel(x_ref, g_ref, sh_ref, sc_ref, wq_ref, wkv_ref, qp_ref, kt_ref, vt_ref, kvb_ref):
    d = x_ref.shape[1]
    npairs = d // PAIR
    h = _normmod(x_ref[...], g_ref[...], sh_ref[...], sc_ref[...]).astype(BF16)
    r = _dot(h, wq_ref[...])
    for t in range(npairs):
        qp_ref[t] = r[:, t * PAIR:(t + 1) * PAIR].astype(BF16)
    rt = lax.dot_general(wkv_ref[...], h, NT, preferred_element_type=F32)
    for t in range(d // HEAD_DIM):
        kt_ref[t] = rt[t * HEAD_DIM:(t + 1) * HEAD_DIM, :]
        vt_ref[t] = rt[d + t * HEAD_DIM:d + (t + 1) * HEAD_DIM, :]
    for t in range(2 * npairs):
        kvb_ref[t // npairs, t % npairs] = rt[t * PAIR:(t + 1) * PAIR, :].astype(BF16)


def qkv_prompt(x, g_all, g_idx, mod, sh_col, sc_col, w_q, w_kv_t, batch, tm):
    m, d = x.shape
    s = m // batch
    bpg = s // tm
    nh = d // HEAD_DIM
    npairs = d // PAIR
    return pl.pallas_call(
        _qkv_kernel,
        grid=(m // tm,),
        in_specs=[pl.BlockSpec((tm, d), lambda i: (i, 0)),
                  _g_spec(d, g_idx), _mod_spec(mod, d, sh_col, bpg), _mod_spec(mod, d, sc_col, bpg),
                  pl.BlockSpec((d, d), lambda i: (0, 0)),
                  pl.BlockSpec((2 * d, d), lambda i: (0, 0))],
        out_specs=[pl.BlockSpec((None, npairs, tm, PAIR), lambda i: (i // bpg, 0, i % bpg, 0)),
                   pl.BlockSpec((None, nh, HEAD_DIM, tm), lambda i: (i // bpg, 0, 0, i % bpg)),
                   pl.BlockSpec((None, nh, HEAD_DIM, tm), lambda i: (i // bpg, 0, 0, i % bpg)),
                   pl.BlockSpec((None, 2, npairs, PAIR, tm), lambda i: (i // bpg, 0, 0, 0, i % bpg))],
        out_shape=[jax.ShapeDtypeStruct((batch, npairs, s, PAIR), BF16),
                   jax.ShapeDtypeStruct((batch, nh, HEAD_DIM, s), F32),
                   jax.ShapeDtypeStruct((batch, nh, HEAD_DIM, s), F32),
                   jax.ShapeDtypeStruct((batch, 2, npairs, PAIR, s), BF16)],
        compiler_params=_params("arbitrary"),
        name="qkv_prompt",
    )(x, g_all, mod, mod, w_q, w_kv_t)


def _sb_prompt_kernel(qi_ref, kj_ref, bias_ref, q_ref, kt_ref, vt_ref, o_ref, acc, tail, *, t):
    s = pl.program_id(1)
    i = qi_ref[s]
    j = kj_ref[s]
    first = j == i
    npairs = q_ref.shape[0]

    @pl.when(first)
    def _():
        acc[...] = jnp.zeros_like(acc)
        tail[...] = jnp.zeros_like(tail)

    row = lax.broadcasted_iota(jnp.int32, (t, t), 0)
    col = lax.broadcasted_iota(jnp.int32, (t, t), 1)
    tri = (row > col).astype(BF16)
    low = lax.broadcasted_iota(jnp.int32, (1, PAIR), 1) < HEAD_DIM
    top = lax.broadcasted_iota(jnp.int32, (2 * t, 1), 0) < t
    causal2 = jnp.concatenate([col < row] * 2, axis=0)
    GO, GD, UD = min(4, npairs), min(2, npairs), True

    def stacked_logits(hp, diagonal):
        q2 = q_ref[hp]
        zero = jnp.zeros_like(q2)
        qs = jnp.concatenate([jnp.where(low, q2, zero), jnp.where(low, zero, q2)], axis=0)
        bias = jnp.where(top, bias_ref[2 * hp], bias_ref[2 * hp + 1])
        sp, lsig = _sb_logits(_dot(qs, kt_ref[hp]) + bias)
        if diagonal:
            sp = jnp.where(causal2, sp, 0.0)
        return sp, lsig

    def group_body(hg, carry, *, diagonal, size):
        parts = [stacked_logits(size * hg + u, diagonal) for u in range(size)]
        later_all = _dot(jnp.concatenate([sp for sp, _ in parts], axis=0).astype(BF16), tri)
        for u, (sp, lsig) in enumerate(parts):
            hp = size * hg + u
            later = later_all[u * 2 * t:(u + 1) * 2 * t]
            tl = tail[hp]
            w = jnp.exp2(lsig - later - jnp.concatenate([tl] * (t // LANES), axis=1))
            if diagonal:
                w = jnp.where(causal2, w, 0.0)
            ov = lax.dot_general(w.astype(BF16), vt_ref[hp], NT, preferred_element_type=F32)
            acc[hp] = acc[hp] + jnp.where(low, ov[:t], ov[t:])
            tail[hp] = tl + (later[:, 0:1] + sp[:, 0:1])
        return carry

    @pl.when(first)
    def _():
        lax.fori_loop(0, npairs // GD, functools.partial(group_body, diagonal=True, size=GD), 0, unroll=UD)

    @pl.when(jnp.logical_not(first))
    def _():
        lax.fori_loop(0, npairs // GO, functools.partial(group_body, diagonal=False, size=GO), 0, unroll=True)

    @pl.when(j == 0)
    def _():
        o_ref[...] = acc[...].astype(BF16)


def sb_attention_prompt(q_pairs, kv_t, bias, t=256):
    batch, npairs, s, _ = q_pairs.shape
    nq = s // t
    qi, kj = [], []
    for i in range(nq):
        for j in range(i, -1, -1):
            qi.append(i)
            kj.append(j)
    qi = jnp.asarray(qi, jnp.int32)
    kj = jnp.asarray(kj, jnp.int32)
    blk = (None, npairs, t, PAIR)
    blk_t = (None, None, npairs, PAIR, t)
    grid_spec = pltpu.PrefetchScalarGridSpec(
        num_scalar_prefetch=2,
        grid=(batch, qi.shape[0]),
        in_specs=[pl.BlockSpec(memory_space=pltpu.SMEM),
                  pl.BlockSpec(blk, lambda b, s_, qi_, kj_: (b, 0, qi_[s_], 0)),
                  pl.BlockSpec(blk_t, lambda b, s_, qi_, kj_: (b, 0, 0, 0, kj_[s_])),
                  pl.BlockSpec(blk_t, lambda b, s_, qi_, kj_: (b, 1, 0, 0, kj_[s_]))],
        out_specs=pl.BlockSpec(blk, lambda b, s_, qi_, kj_: (b, 0, qi_[s_], 0)),
        scratch_shapes=[pltpu.VMEM((npairs, t, PAIR), F32), pltpu.VMEM((npairs, 2 * t, LANES), F32)],
    )
    return pl.pallas_call(
        functools.partial(_sb_prompt_kernel, t=t),
        grid_spec=grid_spec,
        out_shape=jax.ShapeDtypeStruct((batch, npairs, s, PAIR), BF16),
        compiler_params=_params("arbitrary", "arbitrary"),
        name="sb_prompt",
    )(qi, kj, bias, q_pairs, kv_t, kv_t)


def _sb_decode_kernel(pt_ref, bias_ref, qc_ref, knc_ref, vnc_ref, *rest, pages, past_len):
    k_refs = rest[:pages]
    v_refs = rest[pages:2 * pages]
    o_ref, acc, tail, zw = rest[2 * pages:]
    g = pl.program_id(1)
    nh, hd, page = acc.shape

    @pl.when(g == 0)
    def _():
        lane0 = lax.broadcasted_iota(jnp.int32, (1, page), 1) == 0
        pos = past_len + lax.broadcasted_iota(jnp.int32, (1, page), 1) * 0
        vis = pos < pos

        def init_head(h, carry):
            z = jnp.sum(qc_ref[h] * knc_ref[h], axis=0, keepdims=True) + bias_ref[h]
            sp, lsig = _sb_logits(z)
            acc[h] = jnp.where(vis & lane0, jnp.exp2(lsig), 0.0) * vnc_ref[h]
            tail[h] = jnp.broadcast_to(jnp.where(vis, sp, 0.0), (pages, page))
            return carry

        lax.fori_loop(0, nh, init_head, 0)

    def logits_body(h, carry):
        qh = qc_ref[h]
        rows = [jnp.sum(k_refs[p][h] * qh, axis=0, keepdims=True) for p in range(pages)]
        zw[h] = jnp.concatenate(rows, axis=0) + bias_ref[h]
        return carry

    lax.fori_loop(0, nh, logits_body, 0, unroll=2)

    r_ = lax.broadcasted_iota(jnp.int32, (page, page), 0)
    c_ = lax.broadcasted_iota(jnp.int32, (page, page), 1)
    tri = (r_ > c_).astype(BF16)
    slot = lax.broadcasted_iota(jnp.int32, (nh * pages, page), 0) % pages
    sp, lsig = _sb_logits(zw[...].reshape(nh * pages, page))
    within = _split_dot(sp, tri)
    rs = jnp.broadcast_to(jnp.sum(sp, axis=1, keepdims=True), sp.shape)
    run = rs
    sft = 1
    while sft < pages:
        run = run + jnp.where(slot >= sft, pltpu.roll(run, sft, 0), 0.0)
        sft *= 2
    tl = tail[...].reshape(nh * pages, page)
    zw[...] = jnp.exp2(lsig - within - (run - rs) - tl).reshape(nh, pages, page)
    total = run.reshape(nh, pages, page)[:, pages - 1:pages, :]
    tail[...] = tail[...] + jnp.broadcast_to(total, (nh, pages, page))

    def values_body(h, carry):
        w = zw[h]
        a = acc[h]
        for p in range(pages):
            a = a + w[p:p + 1, :] * v_refs[p][h]
        acc[h] = a
        return carry

    lax.fori_loop(0, nh, values_body, 0, unroll=2)

    @pl.when(g == pl.num_programs(1) - 1)
    def _():
        o_ref[...] = jnp.sum(acc[...], axis=2)


def sb_attention_decode(q, k_new, v_new, bias, cache_k, cache_v, page_table, pages=8):
    b, d = q.shape
    n_phys, page, nh, hd = cache_k.shape
    n_pages = page_table.shape[1]
    assert page == LANES and pages == 8 and n_pages % pages == 0
    ck = jnp.transpose(cache_k, (0, 2, 3, 1))
    cv = jnp.transpose(cache_v, (0, 2, 3, 1))
    blk = (None, nh, hd, page)
    seq = pl.BlockSpec(blk, lambda i, g, pt: (i, 0, 0, 0))

    def page_spec(p):
        return pl.BlockSpec(blk, lambda i, g, pt: (pt[i, n_pages - 1 - (g * pages + p)], 0, 0, 0))

    def cols(a):
        return jnp.broadcast_to(a.reshape(b, nh, hd, 1), (b, nh, hd, page))

    grid_spec = pltpu.PrefetchScalarGridSpec(
        num_scalar_prefetch=1,
        grid=(b, n_pages // pages),
        in_specs=[pl.BlockSpec(memory_space=pltpu.SMEM), seq, seq, seq]
        + [page_spec(p) for p in range(pages)] * 2,
        out_specs=pl.BlockSpec((None, nh, hd), lambda i, g, pt: (i, 0, 0)),
        scratch_shapes=[pltpu.VMEM((nh, hd, page), F32), pltpu.VMEM((nh, pages, page), F32),
                        pltpu.VMEM((nh, pages, page), F32)],
    )
    out = pl.pallas_call(
        functools.partial(_sb_decode_kernel, pages=pages, past_len=n_pages * page),
        grid_spec=grid_spec,
        out_shape=jax.ShapeDtypeStruct((b, nh, hd), F32),
        compiler_params=_params("arbitrary", "arbitrary"),
        name="sb_decode",
    )(page_table, bias.astype(F32), cols(q), cols(k_new), cols(v_new), *([ck] * pages), *([cv] * pages))
    return out.reshape(b, d)


def _proj_post_kernel(a_ref, w_ref, x_ref, g_ref, gate_ref, o_ref, *, paired):
    if paired:
        a = jnp.concatenate([a_ref[t] for t in range(a_ref.shape[0])], axis=1)
    else:
        a = a_ref[...]
    y = _dot(a.astype(BF16), w_ref[...])
    o_ref[...] = x_ref[...] + gate_ref[...] * _rms(y, g_ref[...])


def proj_post(a, w, x, g_all, g_idx, mod, gate_col, tm):
    m, d = x.shape
    paired = a.ndim == 4
    bpg = (m // tm) // mod.shape[0]
    if paired:
        spb = a.shape[2] // tm
        a_spec = pl.BlockSpec((None, a.shape[1], tm, PAIR), lambda i: (i // spb, 0, i % spb, 0))
    else:
        a_spec = pl.BlockSpec((tm, a.shape[1]), lambda i: (i, 0))
    return pl.pallas_call(
        functools.partial(_proj_post_kernel, paired=paired),
        grid=(m // tm,),
        in_specs=[a_spec, pl.BlockSpec(w.shape, lambda i: (0, 0)), pl.BlockSpec((tm, d), lambda i: (i, 0)),
                  _g_spec(d, g_idx), _mod_spec(mod, d, gate_col, bpg)],
        out_specs=pl.BlockSpec((tm, d), lambda i: (i, 0)),
        out_shape=jax.ShapeDtypeStruct((m, d), F32),
        compiler_params=_params("arbitrary"),
        name="proj_post",
    )(a, w, x, g_all, mod)


def _ffn_kernel(x_ref, g_ref, sh_ref, sc_ref, wg_ref, wu_ref, wd_ref, gp_ref, gate_ref, o_ref, h_scr, acc):
    f = pl.program_id(1)

    @pl.when(f == 0)
    def _():
        h_scr[...] = _normmod(x_ref[...], g_ref[...], sh_ref[...], sc_ref[...]).astype(BF16)
        acc[...] = jnp.zeros_like(acc)

    h = h_scr[...]
    act = _silu(_dot(h, wg_ref[...])) * _dot(h, wu_ref[...])
    acc[...] += _dot(act.astype(BF16), wd_ref[...])

    @pl.when(f == pl.num_programs(1) - 1)
    def _():
        o_ref[...] = x_ref[...] + gate_ref[...] * _rms(acc[...], gp_ref[...])


def ffn_sublayer(x, g_all, g_pre, g_post, mod, sh_col, sc_col, gate_col, w_gu, w_dn, tm, fc):
    m, d = x.shape
    dff = w_dn.shape[0]
    nf = dff // fc
    bpg = (m // tm) // mod.shape[0]
    wmode = dict(pipeline_mode=pl.Buffered(1)) if nf == 1 else {}
    return pl.pallas_call(
        _ffn_kernel,
        grid=(m // tm, nf),
        in_specs=[pl.BlockSpec((tm, d), lambda i, f: (i, 0)),
                  _g_spec(d, g_pre), _mod_spec(mod, d, sh_col, bpg), _mod_spec(mod, d, sc_col, bpg),
                  pl.BlockSpec((d, fc), lambda i, f: (0, f), **wmode),
                  pl.BlockSpec((d, fc), lambda i, f: (0, nf + f), **wmode),
                  pl.BlockSpec((fc, d), lambda i, f: (f, 0), **wmode),
                  _g_spec(d, g_post), _mod_spec(mod, d, gate_col, bpg)],
        out_specs=pl.BlockSpec((tm, d), lambda i, f: (i, 0)),
        out_shape=jax.ShapeDtypeStruct((m, d), F32),
        scratch_shapes=[pltpu.VMEM((tm, d), BF16), pltpu.VMEM((tm, d), F32)],
        compiler_params=_params("arbitrary", "arbitrary"),
        name="ffn",
    )(x, g_all, mod, mod, w_gu, w_gu, w_dn, g_all, mod)


def _expand_heads(a, e_bf16):
    return _split_dot(a, e_bf16)


def _ssd_prompt_kernel(z_ref, x_ref, bc_ref, dt_ref, cwx_ref, cwbc_ref, cbx_ref, cbbc_ref, dtb_ref, a_ref,
                       dsk_ref, ng_ref, e_ref, y_ref, st_ref, conv_ref, xs_x, xs_bc, y_scr,
                       *, chunk, groups, dstate):
    c = pl.program_id(1)
    last = c == pl.num_programs(1) - 1
    L = chunk
    d_inner = x_ref.shape[1]
    npairs = d_inner // PAIR
    ppg = npairs // groups
    kw = cwx_ref.shape[0]

    @pl.when(c == 0)
    def _():
        xs_x[0:8, :] = jnp.zeros((8, xs_x.shape[1]), F32)
        xs_bc[0:8, :] = jnp.zeros((8, xs_bc.shape[1]), F32)
        st_ref[...] = jnp.zeros_like(st_ref)

    def conv_act(cur_ref, xs, cw_ref, cb_ref):
        xs[8:8 + L, :] = cur_ref[...]
        out = cb_ref[...] + cw_ref[kw - 1:kw, :] * cur_ref[...]
        for k in range(kw - 1):
            sft = kw - 1 - k
            out = out + cw_ref[k:k + 1, :] * xs[8 - sft:8 - sft + L, :]
        return _silu(out)

    x_act = conv_act(x_ref, xs_x, cwx_ref, cbx_ref)
    bc_act = conv_act(bc_ref, xs_bc, cwbc_ref, cbbc_ref)

    @pl.when(last)
    def _():
        conv_ref[:, 0:d_inner] = xs_x[8 + L - (kw - 1):8 + L, :]
        conv_ref[:, d_inner:] = xs_bc[8 + L - (kw - 1):8 + L, :]

    xs_x[0:8, :] = xs_x[L:L + 8, :]
    xs_bc[0:8, :] = xs_bc[L:L + 8, :]

    dt = _softplus(dt_ref[...] + dtb_ref[...])
    dta = dt * a_ref[...]
    r_ = lax.broadcasted_iota(jnp.int32, (L, L), 0)
    c_ = lax.broadcasted_iota(jnp.int32, (L, L), 1)
    causal = r_ >= c_
    a_cs = _dot_hi(causal.astype(F32), dta)
    a_cs_t = a_cs.T
    a_end = a_cs[L - 1:L, :]
    e = e_ref[...]
    dt_x = _expand_heads(dt, e)
    grow_x = _expand_heads(jnp.exp(a_cs), e)
    dend_x = _expand_heads(jnp.exp(a_end - a_cs), e)
    cdec = jnp.exp(a_end)

    low = lax.broadcasted_iota(jnp.int32, (1, PAIR), 1) < HEAD_DIM
    rlow = lax.broadcasted_iota(jnp.int32, (PAIR, 1), 0) < HEAD_DIM
    gw = groups * dstate
    for g in range(groups):
        bg = bc_act[:, g * dstate:(g + 1) * dstate].astype(BF16)
        cg = bc_act[:, gw + g * dstate:gw + (g + 1) * dstate].astype(BF16)
        cb = lax.dot_general(cg, bg, NT, preferred_element_type=F32)
        for pp in range(ppg):
            hp = g * ppg + pp
            sl = slice(hp * PAIR, (hp + 1) * PAIR)
            xdt = x_act[:, sl] * dt_x[:, sl]
            xdt_b = xdt.astype(BF16)
            y = jnp.zeros((L, PAIR), F32)
            for hh in range(2):
                h = 2 * hp + hh
                seg = a_cs[:, h:h + 1] - a_cs_t[h:h + 1, :]
                dec = jnp.where(causal, jnp.exp(seg), 0.0)
                sel = low if hh == 0 else jnp.logical_not(low)
                y = y + _dot((cb * dec).astype(BF16), jnp.where(sel, xdt_b, jnp.zeros_like(xdt_b)))
            prev = st_ref[hp]
            y = y + lax.dot_general(cg, prev.astype(BF16), NT, preferred_element_type=F32) * grow_x[:, sl]
            y_scr[:, sl] = y
            xw = (xdt * dend_x[:, sl]).astype(BF16)
            s_c = lax.dot_general(xw, bg, TN, preferred_element_type=F32)
            cd = jnp.where(rlow, cdec[:, 2 * hp:2 * hp + 1], cdec[:, 2 * hp + 1:2 * hp + 2])
            st_ref[hp] = prev * cd + s_c

    y = y_scr[...] + x_act * dsk_ref[...]
    y = y * _silu(z_ref[...])
    gsz = d_inner // groups
    for g in range(groups):
        sl = slice(g * gsz, (g + 1) * gsz)
        y_ref[:, sl] = _rms(y[:, sl], ng_ref[:, sl]).astype(BF16)


def _head_expander(nheads):
    lane = jnp.arange(nheads * HEAD_DIM)[None, :] // HEAD_DIM
    return (lane == jnp.arange(LANES)[:, None]).astype(BF16)


def ssd_prompt(zx, dt_raw, conv_w, conv_b, dt_bias, a_neg, d_skip, norm_g, batch, chunk, groups, dstate):
    m = zx.shape[0]
    d_inner = d_skip.shape[0] * HEAD_DIM
    nheads = d_skip.shape[0]
    bcw = 2 * groups * dstate
    s = m // batch
    nc = s // chunk
    kw = conv_w.shape[0]
    npairs = d_inner // PAIR
    pad = LANES - nheads
    dtb = jnp.pad(dt_bias.astype(F32), (0, pad)).reshape(1, LANES)
    a_p = jnp.pad(a_neg, (0, pad)).reshape(1, LANES)
    dsk = jnp.repeat(d_skip.astype(F32), HEAD_DIM).reshape(1, d_inner)
    cb2 = conv_b.reshape(1, -1)
    rb = lambda b, c: b * nc + c
    const = lambda shape, j=0: pl.BlockSpec(shape, lambda b, c: (0, j))
    return pl.pallas_call(
        functools.partial(_ssd_prompt_kernel, chunk=chunk, groups=groups, dstate=dstate),
        grid=(batch, nc),
        in_specs=[pl.BlockSpec((chunk, d_inner), lambda b, c: (rb(b, c), 0)),
                  pl.BlockSpec((chunk, d_inner), lambda b, c: (rb(b, c), 1)),
                  pl.BlockSpec((chunk, bcw), lambda b, c: (rb(b, c), 2 * d_inner // bcw)),
                  pl.BlockSpec((chunk, LANES), lambda b, c: (rb(b, c), 0)),
                  const((kw, d_inner)), const((kw, bcw), d_inner // bcw),
                  const((1, d_inner)), const((1, bcw), d_inner // bcw),
                  const((1, LANES)), const((1, LANES)), const((1, d_inner)), const((1, d_inner)),
                  const((LANES, d_inner))],
        out_specs=[pl.BlockSpec((chunk, d_inner), lambda b, c: (rb(b, c), 0)),
                   pl.BlockSpec((None, npairs, PAIR, dstate), lambda b, c: (b, 0, 0, 0)),
                   pl.BlockSpec((None, kw - 1, d_inner + bcw), lambda b, c: (b, 0, 0))],
        out_shape=[jax.ShapeDtypeStruct((m, d_inner), BF16),
                   jax.ShapeDtypeStruct((batch, npairs, PAIR, dstate), F32),
                   jax.ShapeDtypeStruct((batch, kw - 1, d_inner + bcw), F32)],
        scratch_shapes=[pltpu.VMEM((chunk + 8, d_inner), F32), pltpu.VMEM((chunk + 8, bcw), F32),
                        pltpu.VMEM((chunk, d_inner), F32)],
        compiler_params=_params("arbitrary", "arbitrary"),
        name="ssd_prompt",
    )(zx, zx, zx, dt_raw, conv_w, conv_w, cb2, cb2, dtb, a_p, dsk, norm_g.reshape(1, d_inner),
      _head_expander(nheads))


def _ssd_step_kernel(z_ref, x_ref, bc_ref, dt_ref, sx_ref, sbc_ref, st_ref, cwx_ref, cwbc_ref, cbx_ref, cbbc_ref,
                     dtb_ref, a_ref, dsk_ref, ng_ref, y_ref, sto_ref, conv_ref, y_scr, *, groups, dstate):
    d_inner = x_ref.shape[1]
    npairs = d_inner // PAIR
    ppg = npairs // groups
    kw = cwx_ref.shape[0]

    def conv_act(new_ref, s_ref, cw_ref, cb_ref):
        out = cb_ref[...] + cw_ref[kw - 1:kw, :] * new_ref[...]
        for k in range(kw - 1):
            out = out + cw_ref[k:k + 1, :] * s_ref[k:k + 1, :]
        return _silu(out)

    x_act = conv_act(x_ref, sx_ref, cwx_ref, cbx_ref)
    bc_act = conv_act(bc_ref, sbc_ref, cwbc_ref, cbbc_ref)
    conv_ref[0:kw - 2, 0:d_inner] = sx_ref[1:kw - 1, :]
    conv_ref[0:kw - 2, d_inner:] = sbc_ref[1:kw - 1, :]
    conv_ref[kw - 2:kw - 1, 0:d_inner] = x_ref[...]
    conv_ref[kw - 2:kw - 1, d_inner:] = bc_ref[...]

    dt = _softplus(dt_ref[...] + dtb_ref[...])
    decay = jnp.exp(dt * a_ref[...])
    rlow = lax.broadcasted_iota(jnp.int32, (PAIR, 1), 0) < HEAD_DIM
    low = lax.broadcasted_iota(jnp.int32, (1, PAIR), 1) < HEAD_DIM
    eye = lax.broadcasted_iota(jnp.int32, (PAIR, PAIR), 0) == lax.broadcasted_iota(jnp.int32, (PAIR, PAIR), 1)
    gw = groups * dstate
    for hp in range(npairs):
        g = hp // ppg
        sl = slice(hp * PAIR, (hp + 1) * PAIR)
        bg = jnp.broadcast_to(bc_act[:, g * dstate:(g + 1) * dstate], (PAIR, dstate))
        cg = jnp.broadcast_to(bc_act[:, gw + g * dstate:gw + (g + 1) * dstate], (8, dstate))
        dt_p = jnp.where(low, dt[:, 2 * hp:2 * hp + 1], dt[:, 2 * hp + 1:2 * hp + 2])
        xdt = x_act[:, sl] * dt_p
        outer = _dot_hi(jnp.where(eye, jnp.broadcast_to(xdt, (PAIR, PAIR)), 0.0), bg)
        dc = jnp.where(rlow, decay[:, 2 * hp:2 * hp + 1], decay[:, 2 * hp + 1:2 * hp + 2])
        new = st_ref[hp] * dc + outer
        sto_ref[hp] = new
        y_scr[:, sl] = _dot_hi(cg, new, NT)[0:1, :]

    y = y_scr[...] + x_act * dsk_ref[...]
    y = y * _silu(z_ref[...])
    gsz = d_inner // groups
    for g in range(groups):
        sl = slice(g * gsz, (g + 1) * gsz)
        y_ref[:, sl] = _rms(y[:, sl], ng_ref[:, sl]).astype(BF16)


def ssd_step(zx, dt_raw, state_conv, state_ssm, conv_w, conv_b, dt_bias, a_neg, d_skip, norm_g, groups, dstate):
    b = zx.shape[0]
    nheads = d_skip.shape[0]
    d_inner = nheads * HEAD_DIM
    bcw = 2 * groups * dstate
    kw = conv_w.shape[0]
    npairs = d_inner // PAIR
    pad = LANES - nheads
    dtb = jnp.pad(dt_bias.astype(F32), (0, pad)).reshape(1, LANES)
    a_p = jnp.pad(a_neg, (0, pad)).reshape(1, LANES)
    dsk = jnp.repeat(d_skip.astype(F32), HEAD_DIM).reshape(1, d_inner)
    cb2 = conv_b.reshape(1, -1)
    zx3 = zx.reshape(b, 1, -1)
    st = state_ssm.reshape(b, npairs, PAIR, dstate)
    const = lambda shape, j=0: pl.BlockSpec(shape, lambda i: (0, j))
    y, st_new, conv_new = pl.pallas_call(
        functools.partial(_ssd_step_kernel, groups=groups, dstate=dstate),
        grid=(b,),
        in_specs=[pl.BlockSpec((None, 1, d_inner), lambda i: (i, 0, 0)),
                  pl.BlockSpec((None, 1, d_inner), lambda i: (i, 0, 1)),
                  pl.BlockSpec((None, 1, bcw), lambda i: (i, 0, 2 * d_inner // bcw)),
                  pl.BlockSpec((None, 1, LANES), lambda i: (i, 0, 0)),
                  pl.BlockSpec((None, kw - 1, d_inner), lambda i: (i, 0, 0)),
                  pl.BlockSpec((None, kw - 1, bcw), lambda i: (i, 0, d_inner // bcw)),
                  pl.BlockSpec((None, npairs, PAIR, dstate), lambda i: (i, 0, 0, 0)),
                  const((kw, d_inner)), const((kw, bcw), d_inner // bcw),
                  const((1, d_inner)), const((1, bcw), d_inner // bcw),
                  const((1, LANES)), const((1, LANES)), const((1, d_inner)), const((1, d_inner))],
        out_specs=[pl.BlockSpec((None, 1, d_inner), lambda i: (i, 0, 0)),
                   pl.BlockSpec((None, npairs, PAIR, dstate), lambda i: (i, 0, 0, 0)),
                   pl.BlockSpec((None, kw - 1, d_inner + bcw), lambda i: (i, 0, 0))],
        out_shape=[jax.ShapeDtypeStruct((b, 1, d_inner), BF16),
                   jax.ShapeDtypeStruct((b, npairs, PAIR, dstate), F32),
                   jax.ShapeDtypeStruct((b, kw - 1, d_inner + bcw), F32)],
        scratch_shapes=[pltpu.VMEM((1, d_inner), F32)],
        compiler_params=_params("arbitrary"),
        name="ssd_step",
    )(zx3, zx3, zx3, dt_raw.reshape(b, 1, LANES), state_conv, state_conv, st, conv_w, conv_w, cb2, cb2,
      dtb, a_p, dsk, norm_g.reshape(1, d_inner))
    return y.reshape(b, d_inner), st_new, conv_new


def _store_token_rows(ref, val):
    n, d = val.shape
    r = d // LANES
    for c in range(r):
        ref[pl.ds(c, n, stride=r), :] = val[:, c * LANES:(c + 1) * LANES]


def _load_token_rows(ref, n, lead=()):
    r = ref.shape[-2] // n
    return jnp.concatenate([ref[(*lead, pl.ds(c, n, stride=r), slice(None))] for c in range(r)], axis=1)


def _token_rows(ref, t, r):
    return ref.at[pl.ds(pl.multiple_of(t * r, r), r)]


def _router_kernel(x_ref, g_ref, sh_ref, sc_ref, wr_ref, h_ref, route_ref, *, n_experts):
    h = _normmod(x_ref[...], g_ref[...], sh_ref[...], sc_ref[...])
    _store_token_rows(h_ref, h)
    logits = _dot_hi(h, wr_ref[...])
    lane = lax.broadcasted_iota(jnp.int32, logits.shape, 1).astype(F32)
    neg = jnp.float32(-jnp.inf)
    lg = jnp.where(lane < n_experts, logits, neg)
    m1 = jnp.max(lg, axis=1, keepdims=True)
    i1 = jnp.min(jnp.where(lg == m1, lane, float(LANES)), axis=1, keepdims=True)
    lg2 = jnp.where(lane == i1, neg, lg)
    m2 = jnp.max(lg2, axis=1, keepdims=True)
    i2 = jnp.min(jnp.where(lg2 == m2, lane, float(LANES)), axis=1, keepdims=True)
    e2 = jnp.exp(m2 - m1)
    g1 = 1.0 / (1.0 + e2)
    route_ref[...] = (jnp.where(lane == 0.0, i1, 0.0) + jnp.where(lane == 1.0, i2, 0.0)
                      + jnp.where(lane == 2.0, g1, 0.0) + jnp.where(lane == 3.0, e2 * g1, 0.0))


def moe_router(x, g_all, g_idx, mod, sh_col, sc_col, w_router, tm):
    m, d = x.shape
    n_experts = w_router.shape[1]
    bpg = (m // tm) // mod.shape[0]
    wr = jnp.pad(w_router.astype(F32), ((0, 0), (0, LANES - n_experts)))
    return pl.pallas_call(
        functools.partial(_router_kernel, n_experts=n_experts),
        grid=(m // tm,),
        in_specs=[pl.BlockSpec((tm, d), lambda i: (i, 0)),
                  _g_spec(d, g_idx), _mod_spec(mod, d, sh_col, bpg), _mod_spec(mod, d, sc_col, bpg),
                  pl.BlockSpec((d, LANES), lambda i: (0, 0))],
        out_specs=[pl.BlockSpec((tm * d // LANES, LANES), lambda i: (i, 0)),
                   pl.BlockSpec((tm, LANES), lambda i: (i, 0))],
        out_shape=[jax.ShapeDtypeStruct((m * d // LANES, LANES), F32), jax.ShapeDtypeStruct((m, LANES), F32)],
        compiler_params=_params("arbitrary"),
        name="moe_router",
    )(x, g_all, mod, mod, wr)


def moe_plan(routes, n_experts, tm):
    t = routes.shape[0]
    e = routes[:, :2].astype(jnp.int32).reshape(-1)
    onehot = (e[:, None] == jnp.arange(n_experts, dtype=jnp.int32)[None, :]).astype(jnp.int32)
    before = jnp.cumsum(onehot, axis=0) - onehot
    counts = jnp.sum(onehot, axis=0)
    padded = (counts + tm - 1) // tm * tm
    ends = jnp.cumsum(padded)
    pos = jnp.sum(onehot * (before + (ends - padded)[None, :]), axis=1)
    nb = -(-(2 * t + n_experts * (tm - 1)) // tm)
    blk_start = jnp.arange(nb, dtype=jnp.int32) * tm
    blk_e = jnp.minimum(jnp.sum((blk_start[:, None] >= ends[None, :]).astype(jnp.int32), axis=1), n_experts - 1)
    live = (blk_start < ends[-1]).astype(jnp.int32)
    blk_e = jnp.where(live == 1, blk_e, blk_e[jnp.maximum(jnp.sum(live) - 1, 0)])
    return pos.astype(jnp.int32), blk_e.astype(jnp.int32), live


def _slot_spec(tb, copies):
    return pl.BlockSpec((None, 1, copies * tb), lambda i, *_: (i, 0, 0), memory_space=pltpu.SMEM)


def _dispatch_kernel(pos_ref, h_ref, buf_in_ref, buf_ref, sem, *, copies, r):
    del buf_in_ref
    n = h_ref.shape[0] // r

    def row_copy(t, c):
        return pltpu.make_async_copy(_token_rows(h_ref, t, r), _token_rows(buf_ref, pos_ref[0, copies * t + c], r), sem)

    def start(t, carry):
        for c in range(copies):
            row_copy(t, c).start(priority=c % 2)
        return carry

    def wait(t, carry):
        for c in range(copies):
            row_copy(t, c).wait()
        return carry

    lax.fori_loop(0, n, start, 0, unroll=4)
    lax.fori_loop(0, n, wait, 0, unroll=4)


def moe_dispatch(h, pos, buf, tb, r, copies=2):
    m = h.shape[0] // r
    return pl.pallas_call(
        functools.partial(_dispatch_kernel, copies=copies, r=r),
        grid=(m // tb,),
        in_specs=[_slot_spec(tb, copies), pl.BlockSpec((tb * r, LANES), lambda i: (i, 0)),
                  pl.BlockSpec(memory_space=pl.ANY)],
        out_specs=pl.BlockSpec(memory_space=pl.ANY),
        out_shape=jax.ShapeDtypeStruct(buf.shape, buf.dtype),
        scratch_shapes=[pltpu.SemaphoreType.DMA(())],
        input_output_aliases={2: 0},
        compiler_params=_params("arbitrary"),
        name="moe_dispatch",
    )(pos.reshape(m // tb, 1, copies * tb), h, buf)


def _moe_experts_kernel(be_ref, live_ref, h_ref, wg_ref, wu_ref, wd_ref, y_ref, hb, acc):
    f = pl.program_id(1)
    live = live_ref[pl.program_id(0)] == 1

    @pl.when(live & (f == 0))
    def _():
        hb[...] = _load_token_rows(h_ref, hb.shape[0]).astype(BF16)
        acc[...] = jnp.zeros_like(acc)

    @pl.when(live)
    def _():
        h = hb[...]
        act = _silu(_dot(h, wg_ref[...])) * _dot(h, wu_ref[...])
        acc[...] += _dot(act.astype(BF16), wd_ref[...])

    @pl.when(f == pl.num_programs(1) - 1)
    def _():
        _store_token_rows(y_ref, jnp.where(live, acc[...], 0.0))


def moe_experts(h_sorted, blk_e, live, w_gu, w_dn, tm, fc):
    d, dff = w_dn.shape[2], w_dn.shape[1]
    r = d // LANES
    p = h_sorted.shape[0] // r
    nf = dff // fc

    def fe(b, f, live_):
        return jnp.where(live_[b] == 1, f, nf - 1)

    grid_spec = pltpu.PrefetchScalarGridSpec(
        num_scalar_prefetch=2,
        grid=(p // tm, nf),
        in_specs=[pl.BlockSpec((tm * r, LANES), lambda b, f, be, lv: (b, 0)),
                  pl.BlockSpec((None, d, fc), lambda b, f, be, lv: (be[b], 0, fe(b, f, lv))),
                  pl.BlockSpec((None, d, fc), lambda b, f, be, lv: (be[b], 0, nf + fe(b, f, lv))),
                  pl.BlockSpec((None, fc, d), lambda b, f, be, lv: (be[b], fe(b, f, lv), 0))],
        out_specs=pl.BlockSpec((tm * r, LANES), lambda b, f, be, lv: (b, 0)),
        scratch_shapes=[pltpu.VMEM((tm, d), BF16), pltpu.VMEM((tm, d), F32)],
    )
    return pl.pallas_call(
        _moe_experts_kernel,
        grid_spec=grid_spec,
        out_shape=jax.ShapeDtypeStruct((p * r, LANES), F32),
        compiler_params=_params("arbitrary", "arbitrary"),
        name="moe_experts",
    )(blk_e, live, h_sorted, w_gu, w_gu, w_dn)


def _moe_combine_kernel(pos_ref, y_hbm, route_ref, x_ref, gp_ref, gate_ref, o_ref, ybuf, sem):
    tb, d = x_ref.shape
    r = d // LANES

    def row_copy(t, c):
        return pltpu.make_async_copy(_token_rows(y_hbm, pos_ref[0, 2 * t + c], r), _token_rows(ybuf.at[c], t, r), sem)

    def start(t, carry):
        for c in range(2):
            row_copy(t, c).start(priority=c % 2)
        return carry

    def wait(t, carry):
        for c in range(2):
            row_copy(t, c).wait()
        return carry

    lax.fori_loop(0, tb, start, 0, unroll=4)
    lax.fori_loop(0, tb, wait, 0, unroll=4)
    rec = route_ref[...]
    lane = lax.broadcasted_iota(jnp.int32, rec.shape, 1)
    g1 = jnp.sum(jnp.where(lane == 2, rec, 0.0), axis=1, keepdims=True)
    g2 = jnp.sum(jnp.where(lane == 3, rec, 0.0), axis=1, keepdims=True)
    y = g1 * _load_token_rows(ybuf, tb, (0,)) + g2 * _load_token_rows(ybuf, tb, (1,))
    o_ref[...] = x_ref[...] + gate_ref[...] * _rms(y, gp_ref[...])


def moe_combine(y_sorted, pos, routes, x, g_all, g_post, mod, gate_col, tb):
    m, d = x.shape
    bpg = (m // tb) // mod.shape[0]
    return pl.pallas_call(
        _moe_combine_kernel,
        grid=(m // tb,),
        in_specs=[_slot_spec(tb, 2), pl.BlockSpec(memory_space=pl.ANY),
                  pl.BlockSpec((tb, LANES), lambda i: (i, 0)), pl.BlockSpec((tb, d), lambda i: (i, 0)),
                  _g_spec(d, g_post), _mod_spec(mod, d, gate_col, bpg)],
        out_specs=pl.BlockSpec((tb, d), lambda i: (i, 0)),
        out_shape=jax.ShapeDtypeStruct((m, d), F32),
        scratch_shapes=[pltpu.VMEM((2, tb * d // LANES, LANES), F32), pltpu.SemaphoreType.DMA(())],
        compiler_params=_params("arbitrary"),
        name="moe_combine",
    )(pos.reshape(m // tb, 1, 2 * tb), y_sorted, routes, x, g_all, mod)


def kernel(x_prompt, x_sample, cache_k, cache_v, state_ssm, state_conv, page_table, c_prompt, c_sample, norm_g, w_ada, b_ada, sb_w_qkv, sb_w_o, sb_logit_bias, ssd_w_in, ssd_conv_w, ssd_conv_b, ssd_dt_bias, ssd_a_log, ssd_d, ssd_norm_g, ssd_w_out, ffn_w_gu, ffn_w_dn, moe_w_router, moe_w_gu, moe_w_dn):
    bp, seq, d = x_prompt.shape
    bs = x_sample.shape[0]
    nh_sb = sb_logit_bias.shape[0]
    nh_ssd = ssd_d.shape[0]
    d_inner = nh_ssd * HEAD_DIM
    dstate = state_ssm.shape[-1]
    groups = (ssd_conv_w.shape[1] - d_inner) // (2 * dstate)
    n_zx = d_inner + ssd_conv_w.shape[1]
    mp = bp * seq
    tm = min(512, seq)
    tm_big = min(1024, seq)

    c_all = jnp.concatenate([c_prompt, c_sample], axis=0)
    c_all = jnp.pad(c_all, ((0, -c_all.shape[0] % 8), (0, 0)))
    ada = adaln_all(c_all, w_ada, b_ada)
    g_all = norm_g.reshape(-1, 1, d)

    log2e = 1.4426950408889634
    qscale = jnp.concatenate([jnp.full((d,), log2e * HEAD_DIM ** -0.5, F32), jnp.ones((2 * d,), F32)])
    sb_bias = sb_logit_bias.astype(F32) * log2e
    w_qkv = (sb_w_qkv * qscale).astype(BF16)
    w_o = sb_w_o.astype(BF16)
    w_gu = ffn_w_gu.astype(BF16)
    w_dn = ffn_w_dn.astype(BF16)
    w_in = ssd_w_in[:, :n_zx].astype(BF16)
    w_dt = jnp.pad(ssd_w_in[:, n_zx:], ((0, 0), (0, LANES - nh_ssd))).astype(BF16)
    w_out = ssd_w_out.astype(BF16)
    mw_gu = moe_w_gu.astype(BF16)
    mw_dn = moe_w_dn.astype(BF16)
    a_neg = -jnp.exp(ssd_a_log.astype(F32))

    xp = x_prompt.reshape(mp, d)
    xs = x_sample.reshape(bs, d)

    mod_p = ada[0, :bp].reshape(bp, 1, 6 * d)
    mod_s = ada[0, bp:bp + bs].reshape(1, bs, 6 * d)
    q_pairs, k_t, v_t, kv_t = qkv_prompt(xp, g_all, 0, mod_p, 0, 1, w_qkv[:, :d], w_qkv[:, d:].T, bp, tm)
    o_p = sb_attention_prompt(q_pairs, kv_t, sb_bias, t=min(256, seq))
    xp = proj_post(o_p, w_o, xp, g_all, 1, mod_p, 2, tm)
    xp = ffn_sublayer(xp, g_all, 2, 3, mod_p, 3, 4, 5, w_gu, w_dn, tm, w_dn.shape[0])

    qkv_s = nm_matmul(xs, g_all, 0, mod_s, 0, 1, w_qkv, bs, 512)
    q_s, k_s, v_s = qkv_s[:, :d], qkv_s[:, d:2 * d], qkv_s[:, 2 * d:]
    o_s = sb_attention_decode(q_s, k_s, v_s, sb_bias, cache_k, cache_v, page_table)
    xs = proj_post(o_s, w_o, xs, g_all, 1, mod_s, 2, bs)
    xs = ffn_sublayer(xs, g_all, 2, 3, mod_s, 3, 4, 5, w_gu, w_dn, bs, w_dn.shape[0] // 2)

    mod_p = ada[1, :bp].reshape(bp, 1, 6 * d)
    mod_s = ada[1, bp:bp + bs].reshape(1, bs, 6 * d)
    zx_p, dt_p = nm_matmul(xp, g_all, 4, mod_p, 0, 1, w_in, tm, n_zx, extra_w=(w_dt,))
    y_p, ssm_p, conv_p = ssd_prompt(zx_p, dt_p, ssd_conv_w, ssd_conv_b, ssd_dt_bias, a_neg, ssd_d, ssd_norm_g,
                                    bp, min(128, seq), groups, dstate)
    xp = proj_post(y_p, w_out, xp, g_all, 5, mod_p, 2, tm)

    zx_s, dt_s = nm_matmul(xs, g_all, 4, mod_s, 0, 1, w_in, bs, 512, extra_w=(w_dt,))
    y_s, ssm_s, conv_s = ssd_step(zx_s, dt_s, state_conv, state_ssm, ssd_conv_w, ssd_conv_b, ssd_dt_bias, a_neg,
                                  ssd_d, ssd_norm_g, groups, dstate)
    xs = proj_post(y_s, w_out, xs, g_all, 5, mod_s, 2, bs)

    n_experts = moe_w_router.shape[1]
    tb = min(256, seq)
    h_p, route_p = moe_router(xp, g_all, 6, mod_p, 3, 4, moe_w_router, tm)
    h_s, route_s = moe_router(xs, g_all, 6, mod_s, 3, 4, moe_w_router, bs)
    pos, blk_e, live = moe_plan(jnp.concatenate([route_p, route_s], axis=0), n_experts, tm)
    rpt = d // LANES
    h_sorted = jnp.zeros((blk_e.shape[0] * tm * rpt, LANES), F32)
    h_sorted = moe_dispatch(h_p, pos[:2 * mp], h_sorted, tb, rpt)
    h_sorted = moe_dispatch(h_s, pos[2 * mp:], h_sorted, bs, rpt)
    y_sorted = moe_experts(h_sorted, blk_e, live, mw_gu, mw_dn, tm, mw_dn.shape[1] // 2)
    xp = moe_combine(y_sorted, pos[:2 * mp], route_p, xp, g_all, 7, mod_p, 5, tb)
    xs = moe_combine(y_sorted, pos[2 * mp:], route_s, xs, g_all, 7, mod_s, 5, bs)

    hd = (nh_sb, d // nh_sb)
    return (xp.reshape(bp, seq, d), xs.reshape(bs, 1, d),
            jnp.transpose(k_t, (0, 3, 1, 2)), jnp.transpose(v_t, (0, 3, 1, 2)),
            k_s.reshape(bs, 1, *hd), v_s.reshape(bs, 1, *hd),
            ssm_p.reshape(bp, nh_ssd, HEAD_DIM, dstate), conv_p,
            ssm_s.reshape(bs, nh_ssd, HEAD_DIM, dstate), conv_s)
```
